```python
import math
import jax
import jax.numpy as jnp
from jax import lax
import numpy as np

D_MODEL = 1024
BATCH = 4
SEQ = 4096
DEPTH = 4

CTX_LEN = 256
GRID_W = 64

RET_HEADS = 4
RET_DK = 256
RET_DV = 256
RET_W = RET_HEADS * RET_DV
RET_CHUNK = 128

CONV_W = D_MODEL
CONV_K = 3

ATT_HEADS = 8
ATT_DH = 64
ATT_W = ATT_HEADS * 2 * ATT_DH
Q_BLOCK = 128

ROPE_BASE = 10000.0
N_BRANCH = 3
LN_EPS = 1e-6
DEEPNORM_ALPHA = (2 * DEPTH) ** 0.25
DEEPNORM_BETA = (8 * DEPTH) ** -0.25

KV_SIZES = (RET_HEADS * RET_DK, RET_W, 2 * ATT_HEADS * ATT_DH, ATT_W)
IN_SIZES = KV_SIZES + (RET_HEADS * RET_DK, RET_W, 2 * ATT_HEADS * ATT_DH, ATT_W, CONV_W, CONV_W, CONV_W, CONV_W, N_BRANCH * D_MODEL)
KV_COLS = RET_HEADS * RET_DK + RET_W + 2 * ATT_HEADS * ATT_DH + ATT_W
IN_COLS = KV_COLS + RET_HEADS * RET_DK + RET_W + 2 * ATT_HEADS * ATT_DH + ATT_W + 4 * CONV_W + N_BRANCH * D_MODEL

kernel_name = "hybrid_retention_conv_diffattn_prefix_dit"


def _split(p, sizes):
    return jnp.split(p, np.cumsum(sizes)[:-1].tolist(), axis=-1)


def _layer_norm(x, gain=None, bias=None):
    xf = x.astype(jnp.float32)
    xc = xf - jnp.mean(xf, axis=-1, keepdims=True)
    y = xc * lax.rsqrt(jnp.mean(xc * xc, axis=-1, keepdims=True) + LN_EPS)
    if gain is not None:
        y = y * gain.astype(jnp.float32) + bias.astype(jnp.float32)
    return y.astype(x.dtype)


def _rms_norm(x):
    xf = x.astype(jnp.float32)
    return (xf * lax.rsqrt(jnp.mean(xf * xf, axis=-1, keepdims=True) + LN_EPS)).astype(x.dtype)


def _axial_rope_tables(rows, cols, head_dim):
    n_freq = head_dim // 4
    inv = ROPE_BASE ** (-jnp.arange(n_freq, dtype=jnp.float32) / n_freq)
    ang = jnp.concatenate([rows[:, None].astype(jnp.float32) * inv,
                           cols[:, None].astype(jnp.float32) * inv], axis=-1)
    return jnp.cos(ang), jnp.sin(ang)


def _apply_rope(x, cos, sin):
    half = x.shape[-1] // 2
    shape = (x.shape[1],) + (1,) * (x.ndim - 3) + (half,)
    c = cos.reshape(shape).astype(x.dtype)
    s = sin.reshape(shape).astype(x.dtype)
    x1, x2 = x[..., :half], x[..., half:]
    return jnp.concatenate([x1 * c - x2 * s, x1 * s + x2 * c], axis=-1)


def _retention_dir(q, k, v, log_g, s0, exclusive):
    b, h, t, dk = q.shape
    dv = v.shape[-1]
    n = t // RET_CHUNK
    qc = q.reshape(b, h, n, RET_CHUNK, dk)
    kc = k.reshape(b, h, n, RET_CHUNK, dk)
    vc = v.reshape(b, h, n, RET_CHUNK, dv)
    i = jnp.arange(RET_CHUNK, dtype=jnp.float32)
    dist = i[:, None] - i[None, :]
    mask = dist > 0 if exclusive else dist >= 0
    dmat = jnp.where(mask, jnp.exp(log_g[:, None, None] * jnp.where(mask, dist, 0.0)), 0.0)
    scores = jnp.einsum('bhncd,bhnsd->bhncs', qc, kc) * dmat[None, :, None]
    o_intra = jnp.einsum('bhncs,bhnse->bhnce', scores, vc)
    k_dec = jnp.exp(log_g[:, None] * (RET_CHUNK - 1 - i)[None, :])
    inc = jnp.einsum('bhncd,hc,bhnce->nbhde', kc, k_dec, vc).astype(jnp.float32)
    g_chunk = jnp.exp(log_g * RET_CHUNK)[None, :, None, None]

    def step(s, inc_n):
        return g_chunk * s + inc_n, s

    _, s_before = lax.scan(step, s0.astype(jnp.float32), inc)
    q_dec = jnp.exp(log_g[:, None] * (i + 1.0)[None, :])
    o_cross = jnp.einsum('bhncd,hc,nbhde->bhnce', qc, q_dec, s_before)
    return (o_intra + o_cross).reshape(b, h, t, dv)


def _retention_state(k, v, log_g, reverse):
    t = k.shape[1]
    pos = jnp.arange(t, dtype=jnp.float32)
    dist = pos if reverse else (t - 1 - pos)
    w = jnp.exp(log_g[:, None] * dist[None, :])
    return jnp.einsum('bthd,ht,bthe->bhde', k, w.astype(k.dtype), v).astype(jnp.float32)


def _retention_branch(q, k, v, g, log_g, s_f, s_b):
    qt, kt, vt = (a.transpose(0, 2, 1, 3) for a in (q, k, v))
    fwd = _retention_dir(qt, kt, vt, log_g[0], s_f, exclusive=False)
    bwd = _retention_dir(qt[:, :, ::-1], kt[:, :, ::-1], vt[:, :, ::-1], log_g[1], s_b, exclusive=True)[:, :, ::-1]
    o = _layer_norm((fwd + bwd).transpose(0, 2, 1, 3))
    return o.reshape(g.shape).astype(g.dtype) * jax.nn.silu(g)


def _conv_branch(gate_b, gate_c, x_in, g, w):
    u = gate_c * x_in
    up = jnp.pad(u, ((0, 0), (1, 1), (0, 0)))
    conv = w[0] * up[:, :-2] + w[1] * up[:, 1:-1] + w[2] * up[:, 2:]
    return gate_b * conv * jax.nn.silu(g)


def _diff_attention(q, k_all, v_all, lam):
    b, t, h, _, d = q.shape
    nb = t // Q_BLOCK
    qb = (q * d ** -0.5).reshape(b, nb, Q_BLOCK, h, 2, d).transpose(1, 0, 2, 3, 4, 5)

    def block(qi):
        s = jnp.einsum('bqhjd,bkhjd->bhjqk', qi, k_all).astype(jnp.float32)
        p = jax.nn.softmax(s, axis=-1)
        a = p[:, :, 0] - lam * p[:, :, 1]
        return jnp.einsum('bhqk,bkhe->bqhe', a.astype(v_all.dtype), v_all)

    o = lax.map(block, qb)
    return o.transpose(1, 0, 2, 3, 4).reshape(b, t, h, -1)


def _diff_branch(q, k_all, v_all, g, lam, lam_init):
    o = _rms_norm(_diff_attention(q, k_all, v_all, lam)) * (1.0 - lam_init)
    return o.reshape(g.shape).astype(g.dtype) * jax.nn.silu(g)


def _merge(y_ret, y_conv, y_att, gate_logits, w_r, w_c, w_a, w_o):
    g_r, g_c, g_a = jnp.split(jax.nn.sigmoid(gate_logits), N_BRANCH, axis=-1)
    m = g_r * (y_ret @ w_r) + g_c * (y_conv @ w_c) + g_a * (y_att @ w_a)
    return m @ w_o


def setup_inputs(seed: int = 0) -> dict:
    key = jax.random.key(seed)
    ks = jax.random.split(key, 16)
    f32 = jnp.float32
    d = D_MODEL

    def nrm(k, shape, s):
        return jax.random.normal(k, shape, f32) * s

    head = jnp.arange(RET_HEADS, dtype=f32)
    decay_init = jnp.log(-jnp.log1p(-jnp.power(2.0, -5.0 - head)))
    return {
        'x': nrm(ks[0], (BATCH, SEQ, d), 1.0),
        'c': nrm(ks[1], (BATCH, d), 1.0),
        'ctx': nrm(ks[2], (BATCH, CTX_LEN, d), 1.0),
        'c_ctx': nrm(ks[3], (d,), 1.0),
        'w_mod': nrm(ks[4], (DEPTH, d, 3 * d), 0.5 * d ** -0.5),
        'b_mod': nrm(ks[5], (DEPTH, 3 * d), 0.02),
        'w_in': nrm(ks[6], (DEPTH, d, IN_COLS), d ** -0.5),
        'ret_decay': decay_init + nrm(ks[7], (DEPTH, 2, RET_HEADS), 0.05),
        'conv_w': nrm(ks[8], (DEPTH, CONV_K, CONV_W), CONV_K ** -0.5),
        'diff_lambda': nrm(ks[9], (DEPTH, 4, ATT_DH), 0.1),
        'w_ret_out': nrm(ks[10], (DEPTH, RET_W, d), RET_W ** -0.5 * DEEPNORM_BETA),
        'w_conv_out': nrm(ks[11], (DEPTH, CONV_W, d), CONV_W ** -0.5 * DEEPNORM_BETA),
        'w_att_out': nrm(ks[12], (DEPTH, ATT_W, d), ATT_W ** -0.5 * DEEPNORM_BETA),
        'w_out': nrm(ks[13], (DEPTH, d, d), d ** -0.5 * DEEPNORM_BETA),
        'ln_g': 1.0 + nrm(ks[14], (DEPTH, d), 0.02),
        'ln_b': nrm(ks[15], (DEPTH, d), 0.02),
    }


def reference(x, c, ctx, c_ctx, w_mod, b_mod, w_in, ret_decay, conv_w, diff_lambda,
              w_ret_out, w_conv_out, w_att_out, w_out, ln_g, ln_b):
    f32 = jnp.float32
    bsz, t_lat, _ = x.shape
    t_ctx = ctx.shape[1]
    n_rows = t_lat // GRID_W
    rows = jnp.repeat(jnp.arange(n_rows, dtype=jnp.int32), GRID_W)
    cols = jnp.tile(jnp.arange(GRID_W, dtype=jnp.int32), n_rows)
    cos_r, sin_r = _axial_rope_tables(rows, cols, RET_DK)
    cos_a, sin_a = _axial_rope_tables(rows, cols, ATT_DH)
    xc = ctx
    for l in range(DEPTH):
        last = l == DEPTH - 1
        lam_init = 0.8 - 0.6 * math.exp(-0.3 * l)
        log_g = -jnp.exp(ret_decay[l].astype(f32))
        lv = diff_lambda[l].astype(f32)
        lam = jnp.exp(jnp.sum(lv[0] * lv[1])) - jnp.exp(jnp.sum(lv[2] * lv[3])) + lam_init

        shift, scale, gate = jnp.split(jax.nn.silu(c) @ w_mod[l] + b_mod[l], 3, axis=-1)
        shift_c, scale_c, gate_c = jnp.split(jax.nn.silu(c_ctx) @ w_mod[l] + b_mod[l], 3, axis=-1)
        h = _layer_norm(x) * (1.0 + scale[:, None]) + shift[:, None]
        hc = _layer_norm(xc) * (1.0 + scale_c) + shift_c

        if last:
            pc = _split(hc @ w_in[l][:, :KV_COLS], KV_SIZES)
        else:
            pc = _split(hc @ w_in[l], IN_SIZES)
        ck_r = pc[0].reshape(bsz, t_ctx, RET_HEADS, RET_DK)
        cv_r = pc[1].reshape(bsz, t_ctx, RET_HEADS, RET_DV)
        ck_a = pc[2].reshape(bsz, t_ctx, ATT_HEADS, 2, ATT_DH)
        cv_a = pc[3].reshape(bsz, t_ctx, ATT_HEADS, 2 * ATT_DH)
        s_f = _retention_state(ck_r, cv_r, log_g[0], reverse=False)
        s_b = _retention_state(ck_r, cv_r, log_g[1], reverse=True)

        rk, rv, ak, av, rq, rg, aq, ag, cb, cc, cx, cg, mg = _split(h @ w_in[l], IN_SIZES)
        q_r = _apply_rope(rq.reshape(bsz, t_lat, RET_HEADS, RET_DK), cos_r, sin_r) * RET_DK ** -0.5
        k_r = _apply_rope(rk.reshape(bsz, t_lat, RET_HEADS, RET_DK), cos_r, sin_r)
        y_ret = _retention_branch(q_r, k_r, rv.reshape(bsz, t_lat, RET_HEADS, RET_DV), rg, log_g, s_f, s_b)
        y_conv = _conv_branch(cb, cc, cx, cg, conv_w[l])
        q_a = _apply_rope(aq.reshape(bsz, t_lat, ATT_HEADS, 2, ATT_DH), cos_a, sin_a)
        k_all = jnp.concatenate([ck_a, _apply_rope(ak.reshape(bsz, t_lat, ATT_HEADS, 2, ATT_DH), cos_a, sin_a)], axis=1)
        v_all = jnp.concatenate([cv_a, av.reshape(bsz, t_lat, ATT_HEADS, 2 * ATT_DH)], axis=1)
        y_att = _diff_branch(q_a, k_all, v_all, ag, lam, lam_init)
        out = _merge(y_ret, y_conv, y_att, mg, w_ret_out[l], w_conv_out[l], w_att_out[l], w_out[l])
        x_new = _layer_norm(DEEPNORM_ALPHA * x + gate[:, None] * out, ln_g[l], ln_b[l])

        if not last:
            crq, crg, caq, cag, ccb, ccc, ccx, ccg, cmg = pc[4:]
            zero = jnp.zeros_like(s_f)
            cq_r = crq.reshape(bsz, t_ctx, RET_HEADS, RET_DK) * RET_DK ** -0.5
            cy_ret = _retention_branch(cq_r, ck_r, cv_r, crg, log_g, zero, zero)
            cy_conv = _conv_branch(ccb, ccc, ccx, ccg, conv_w[l])
            cy_att = _diff_branch(caq.reshape(bsz, t_ctx, ATT_HEADS, 2, ATT_DH), ck_a, cv_a, cag, lam, lam_init)
            out_c = _merge(cy_ret, cy_conv, cy_att, cmg, w_ret_out[l], w_conv_out[l], w_att_out[l], w_out[l])
            xc = _layer_norm(DEEPNORM_ALPHA * xc + gate_c * out_c, ln_g[l], ln_b[l])
        x = x_new
    return x
```

```python
import functools
import math

import jax
import jax.numpy as jnp
from jax import lax
from jax.experimental import pallas as pl
from jax.experimental.pallas import tpu as pltpu

F32 = jnp.float32
BF16 = jnp.bfloat16

D_MODEL = 1024
DEPTH = 4
GRID_W = 64
RET_HEADS = 4
RET_DK = 256
RET_DV = 256
ATT_HEADS = 8
ATT_DH = 64
ROPE_BASE = 10000.0
LN_EPS = 1e-6
DEEPNORM_ALPHA = (2 * DEPTH) ** 0.25

G_RK, G_RV, G_AK, G_AV, G_RQ, G_RG, G_AQ, G_AG, G_CB, G_CC, G_CX, G_CG, G_MR, G_MC, G_MA = range(15)
N_GROUPS = 15
N_KV_GROUPS = 4

LANES = 128
BF16_SUBLANES = 16
VMEM_LIMIT = 56 * 1024 * 1024

RET_CHUNK = 256
PROJ_TM = 1024
ATT_TQ = 512
ATT_TK = 512
MERGE_TM = 256


def _sigmoid(x):
    return 1.0 / (1.0 + jnp.exp(-x))


def _layer_norm_rows(x):
    mu = jnp.mean(x, axis=-1, keepdims=True)
    xc = x - mu
    var = jnp.mean(xc * xc, axis=-1, keepdims=True)
    return xc * lax.rsqrt(var + LN_EPS)


def _params(*sem):
    return pltpu.CompilerParams(dimension_semantics=sem, vmem_limit_bytes=VMEM_LIMIT)


def _mod_kernel(c_ref, w_ref, b_ref, o_ref):
    c = c_ref[...]
    s = (c * _sigmoid(c)).astype(BF16)
    o_ref[...] = jnp.dot(s, w_ref[...].astype(BF16), preferred_element_type=F32) + b_ref[...]


def _modulation(c_rows, w_mod, b_mod):
    depth, d, d3 = w_mod.shape
    tn = 1024
    return pl.pallas_call(
        _mod_kernel,
        grid=(depth, d3 // tn),
        in_specs=[
            pl.BlockSpec((8, d), lambda l, j: (0, 0)),
            pl.BlockSpec((None, d, tn), lambda l, j: (l, 0, j)),
            pl.BlockSpec((None, 1, tn), lambda l, j: (l, 0, j)),
        ],
        out_specs=pl.BlockSpec((None, 8, tn), lambda l, j: (l, 0, j)),
        out_shape=jax.ShapeDtypeStruct((depth, 8, d3), F32),
        compiler_params=_params("arbitrary", "arbitrary"),
        name="modulation",
    )(c_rows, w_mod, b_mod.reshape(depth, 1, d3))


def _proj_kernel(*refs, rope):
    if rope:
        x_ref, sh_ref, sc_ref, w_ref, cr_ref, sr_ref, ca_ref, sa_ref, o_ref, h_ref, acc_ref = refs
    else:
        x_ref, sh_ref, sc_ref, w_ref, o_ref, h_ref, acc_ref = refs
    j = pl.program_id(1)

    @pl.when(j == 0)
    def _():
        y = _layer_norm_rows(x_ref[...])
        h_ref[...] = (y * (1.0 + sc_ref[...]) + sh_ref[...]).astype(BF16)

    acc_ref[...] = jnp.dot(h_ref[...], w_ref[...], preferred_element_type=F32)

    def store_rope_ret(scale):
        c = cr_ref[...]
        s = sr_ref[...]
        for hh in range(RET_HEADS):
            lo = hh * RET_DK
            mid = lo + RET_DK // 2
            hi = lo + RET_DK
            x1 = acc_ref[:, lo:mid]
            x2 = acc_ref[:, mid:hi]
            o_ref[:, lo:mid] = ((x1 * c - x2 * s) * scale).astype(BF16)
            o_ref[:, mid:hi] = ((x1 * s + x2 * c) * scale).astype(BF16)

    def store_rope_att(scale):
        c = ca_ref[...]
        s = sa_ref[...]
        lane = lax.broadcasted_iota(jnp.int32, c.shape, 1)
        first_half = (lane % ATT_DH) < (ATT_DH // 2)
        for hh in range(ATT_HEADS):
            lo = hh * LANES
            x = acc_ref[:, lo:lo + LANES]
            partner = jnp.where(first_half,
                                pltpu.roll(x, LANES - ATT_DH // 2, axis=1),
                                pltpu.roll(x, ATT_DH // 2, axis=1))
            o_ref[:, lo:lo + LANES] = ((x * c + partner * s) * scale).astype(BF16)

    def store_scaled(scale):
        o_ref[...] = (acc_ref[...] * scale).astype(BF16)

    def store_plain():
        o_ref[...] = acc_ref[...].astype(BF16)

    def store_silu():
        a = acc_ref[...]
        o_ref[...] = (a * _sigmoid(a)).astype(BF16)

    def store_sigmoid():
        o_ref[...] = _sigmoid(acc_ref[...]).astype(BF16)

    q_r_scale = RET_DK ** -0.5
    q_a_scale = ATT_DH ** -0.5
    if rope:
        pl.when(j == G_RK)(functools.partial(store_rope_ret, 1.0))
        pl.when(j == G_RQ)(functools.partial(store_rope_ret, q_r_scale))
        pl.when(j == G_AK)(functools.partial(store_rope_att, 1.0))
        pl.when(j == G_AQ)(functools.partial(store_rope_att, q_a_scale))
        plain = (G_RV, G_AV, G_CB, G_CC, G_CX)
    else:
        pl.when(j == G_RQ)(functools.partial(store_scaled, q_r_scale))
        pl.when(j == G_AQ)(functools.partial(store_scaled, q_a_scale))
        plain = (G_RK, G_AK, G_RV, G_AV, G_CB, G_CC, G_CX)
    is_plain = functools.reduce(jnp.logical_or, [j == g for g in plain])
    pl.when(is_plain)(store_plain)
    is_silu = functools.reduce(jnp.logical_or, [j == g for g in (G_RG, G_AG, G_CG)])
    pl.when(is_silu)(store_silu)
    pl.when(j >= G_MR)(store_sigmoid)


def _projection(x2d, mod_rows, mod_row_of_tile, w_bf16, n_groups, seq_len, rope_tabs):
    m, d = x2d.shape
    tm = min(PROJ_TM, seq_len)
    rope = rope_tabs is not None
    tiles_per_seq = seq_len // tm
    in_specs = [
        pl.BlockSpec((tm, d), lambda i, j: (i, 0)),
        pl.BlockSpec((None, 1, d), lambda i, j: (mod_row_of_tile(i), 0, 0)),
        pl.BlockSpec((None, 1, d), lambda i, j: (mod_row_of_tile(i), 0, 1)),
        pl.BlockSpec((d, d), lambda i, j: (0, j)),
    ]
    args = [x2d, mod_rows, mod_rows, w_bf16]
    if rope:
        tab_spec = pl.BlockSpec((tm, LANES), lambda i, j: (i % tiles_per_seq, 0))
        in_specs += [tab_spec] * 4
        args += list(rope_tabs)
    return pl.pallas_call(
        functools.partial(_proj_kernel, rope=rope),
        grid=(m // tm, n_groups),
        in_specs=in_specs,
        out_specs=pl.BlockSpec((None, tm, d), lambda i, j: (j, i, 0)),
        out_shape=jax.ShapeDtypeStruct((n_groups, m, d), BF16),
        scratch_shapes=[pltpu.VMEM((tm, d), BF16), pltpu.VMEM((tm, d), F32)],
        compiler_params=_params("arbitrary", "arbitrary"),
        name="projection_rope" if rope else "projection_ctx",
    )(*args)


def _lanes(x, n):
    return x if n == LANES else jnp.concatenate([x] * (n // LANES), axis=1)


_KDEC_F, _KDEC_B, _QDEC_F, _QDEC_B = range(4)


def _ret_kernel(*refs, seq_len, chunk, zero_init, emit_out, emit_state):
    refs = list(refs)
    k_ref, v_ref = refs.pop(0), refs.pop(0)
    q_ref = g_ref = None
    if emit_out:
        q_ref, g_ref = refs.pop(0), refs.pop(0)
    df_ref, db_ref = refs.pop(0), refs.pop(0)
    sf0_ref = sb0_ref = None
    if not zero_init:
        sf0_ref, sb0_ref = refs.pop(0), refs.pop(0)
    y_ref = sfo_ref = sbo_ref = None
    if emit_out:
        y_ref = refs.pop(0)
    if emit_state:
        sfo_ref, sbo_ref = refs.pop(0), refs.pop(0)
    sf_ref, sb_ref, dec_ref = refs.pop(0), refs.pop(0), refs.pop(0)
    dmat_ref = bwd_ref = None
    if emit_out:
        dmat_ref, bwd_ref = refs.pop(0), refs.pop(0)

    n_chunks = seq_len // chunk
    contract_rows = (((0,), (0,)), ((), ()))
    contract_last = (((1,), (1,)), ((), ()))

    lg_f = _lanes(-jnp.exp(df_ref[...]), RET_DK)
    lg_b = _lanes(-jnp.exp(db_ref[...]), RET_DK)
    pos = lax.broadcasted_iota(jnp.int32, (chunk, RET_DK), 0).astype(F32)
    dec_ref[_KDEC_F] = jnp.exp(lg_f * (chunk - 1.0 - pos))
    dec_ref[_KDEC_B] = jnp.exp(lg_b * pos)
    gc_f = jnp.exp(lg_f * float(chunk))
    gc_b = jnp.exp(lg_b * float(chunk))
    if emit_out:
        dec_ref[_QDEC_F] = jnp.exp(lg_f * (pos + 1.0))
        dec_ref[_QDEC_B] = jnp.exp(lg_b * (float(chunk) - pos))
        row = lax.broadcasted_iota(jnp.int32, (chunk, chunk), 0)
        col = lax.broadcasted_iota(jnp.int32, (chunk, chunk), 1)
        dist = (row - col).astype(F32)
        lgf_c = _lanes(lg_f[:, :LANES], chunk)
        lgb_c = _lanes(lg_b[:, :LANES], chunk)
        dmat_ref[...] = jnp.where(dist >= 0.0,
                                  jnp.exp(lgf_c * jnp.maximum(dist, 0.0)),
                                  jnp.exp(lgb_c * jnp.maximum(-dist, 0.0)))

    if zero_init:
        sf_ref[...] = jnp.zeros_like(sf_ref)
        sb_ref[...] = jnp.zeros_like(sb_ref)
    else:
        sf_ref[...] = sf0_ref[...]
        sb_ref[...] = sb0_ref[...]

    def rows(n):
        return pl.ds(pl.multiple_of(n * chunk, chunk), chunk)

    def state_step(s_ref, kn, vn, kdec, gc):
        kd = (kn.astype(F32) * kdec).astype(BF16)
        s_ref[...] = gc * s_ref[...] + lax.dot_general(kd, vn, contract_rows, preferred_element_type=F32)

    def bwd_body(t, carry):
        n = n_chunks - 1 - t
        r = rows(n)
        kn, vn = k_ref[r, :], v_ref[r, :]
        if emit_out:
            cross = jnp.dot(q_ref[r, :], sb_ref[...].astype(BF16), preferred_element_type=F32)
            bwd_ref[r, :] = cross * dec_ref[_QDEC_B]
        state_step(sb_ref, kn, vn, dec_ref[_KDEC_B], gc_b)
        return carry

    lax.fori_loop(0, n_chunks, bwd_body, 0)

    def fwd_body(n, carry):
        r = rows(n)
        kn, vn = k_ref[r, :], v_ref[r, :]
        if emit_out:
            qn = q_ref[r, :]
            scores = lax.dot_general(qn, kn, contract_last, preferred_element_type=F32) * dmat_ref[...]
            o = jnp.dot(scores.astype(BF16), vn, preferred_element_type=F32)
            o = o + jnp.dot(qn, sf_ref[...].astype(BF16), preferred_element_type=F32) * dec_ref[_QDEC_F]
            o = o + bwd_ref[r, :]
            y_ref[r, :] = (_layer_norm_rows(o) * g_ref[r, :].astype(F32)).astype(BF16)
        state_step(sf_ref, kn, vn, dec_ref[_KDEC_F], gc_f)
        return carry

    lax.fori_loop(0, n_chunks, fwd_body, 0)

    if emit_state:
        sfo_ref[...] = sf_ref[...]
        sbo_ref[...] = sb_ref[...]


def _retention(p, decay, batch, seq_len, states, emit_out, emit_state):
    m = batch * seq_len
    chunk = min(RET_CHUNK, seq_len)

    def group_spec(g):
        return pl.BlockSpec((None, seq_len, RET_DK), lambda b, h: (g, b, h))

    state_spec = pl.BlockSpec((None, None, RET_DK, RET_DV), lambda b, h: (b, h, 0, 0))
    in_specs = [group_spec(G_RK), group_spec(G_RV)]
    args = [p, p]
    if emit_out:
        in_specs += [group_spec(G_RQ), group_spec(G_RG)]
        args += [p, p]
    in_specs += [pl.BlockSpec((None, 1, LANES), lambda b, h: (h, 0, 0)),
                 pl.BlockSpec((None, 1, LANES), lambda b, h: (RET_HEADS + h, 0, 0))]
    args += [decay, decay]
    if states is not None:
        in_specs += [state_spec, state_spec]
        args += list(states)
    out_specs, out_shape = [], []
    if emit_out:
        out_specs.append(pl.BlockSpec((seq_len, RET_DV), lambda b, h: (b, h)))
        out_shape.append(jax.ShapeDtypeStruct((m, RET_HEADS * RET_DV), BF16))
    if emit_state:
        out_specs += [state_spec, state_spec]
        out_shape += [jax.ShapeDtypeStruct((batch, RET_HEADS, RET_DK, RET_DV), F32)] * 2
    scratch = [pltpu.VMEM((RET_DK, RET_DV), F32), pltpu.VMEM((RET_DK, RET_DV), F32),
               pltpu.VMEM((4, chunk, RET_DK), F32)]
    if emit_out:
        scratch += [pltpu.VMEM((chunk, chunk), F32), pltpu.VMEM((seq_len, RET_DV), F32)]
    outs = pl.pallas_call(
        functools.partial(_ret_kernel, seq_len=seq_len, chunk=chunk, zero_init=states is None,
                          emit_out=emit_out, emit_state=emit_state),
        grid=(batch, RET_HEADS),
        in_specs=in_specs,
        out_specs=out_specs,
        out_shape=out_shape,
        scratch_shapes=scratch,
        compiler_params=_params("arbitrary", "arbitrary"),
        name="retention_%d" % seq_len,
    )(*args)
    outs = list(outs)
    y = outs.pop(0) if emit_out else None
    st = tuple(outs) if emit_state else None
    return y, st


def _att_kernel(*refs, n_lat, tk, lam_init):
    refs = list(refs)
    q_ref, kc_ref, vc_ref = refs.pop(0), refs.pop(0), refs.pop(0)
    kl_ref = vl_ref = None
    if n_lat:
        kl_ref, vl_ref = refs.pop(0), refs.pop(0)
    g_ref, lam_ref, o_ref = refs.pop(0), refs.pop(0), refs.pop(0)
    sc_ref = refs.pop(0)
    sl_ref = refs.pop(0) if n_lat else None
    m_ref, l_ref, acc_ref = refs

    tq = q_ref.shape[0]
    t_ctx = kc_ref.shape[0]
    contract_last = (((1,), (1,)), ((), ()))
    q = q_ref[...]
    lane = lax.broadcasted_iota(jnp.int32, q.shape, 1)
    zero = jnp.zeros_like(q)
    qm = (jnp.where(lane < ATT_DH, q, zero), jnp.where(lane >= ATT_DH, q, zero))

    def lane_groups(x):
        return [x[:, g * LANES:(g + 1) * LANES] for g in range(x.shape[1] // LANES)]

    def group_max(x):
        return functools.reduce(jnp.maximum, lane_groups(x))

    def group_sum(x):
        return functools.reduce(jnp.add, lane_groups(x))

    for j in range(2):
        s = lax.dot_general(qm[j], kc_ref[...], contract_last, preferred_element_type=F32)
        sc_ref[j] = s
        m_ref[j] = group_max(s)

    if n_lat:
        def pass_a(c, carry):
            kc = kl_ref[pl.ds(pl.multiple_of(c * tk, tk), tk), :]
            for j in range(2):
                s = lax.dot_general(qm[j], kc, contract_last, preferred_element_type=F32)
                sl_ref[j, c] = s
                m_ref[j] = jnp.maximum(m_ref[j], group_max(s))
            return carry
        lax.fori_loop(0, n_lat, pass_a, 0)

    for j in range(2):
        mx = jnp.max(m_ref[j], axis=-1, keepdims=True)
        m_ref[j] = jnp.broadcast_to(mx, (tq, LANES))

    for j in range(2):
        p = jnp.exp(sc_ref[j] - _lanes(m_ref[j], t_ctx))
        sc_ref[j] = p
        l_ref[j] = group_sum(p)

    if n_lat:
        def pass_b(c, carry):
            for j in range(2):
                p = jnp.exp(sl_ref[j, c] - _lanes(m_ref[j], tk))
                sl_ref[j, c] = p
                l_ref[j] = l_ref[j] + group_sum(p)
            return carry
        lax.fori_loop(0, n_lat, pass_b, 0)

    lv = lam_ref[...]
    lam = (jnp.exp(jnp.sum(lv[0:1, :] * lv[1:2, :], axis=-1, keepdims=True))
           - jnp.exp(jnp.sum(lv[2:3, :] * lv[3:4, :], axis=-1, keepdims=True)) + lam_init)
    l0 = jnp.broadcast_to(jnp.sum(l_ref[0], axis=-1, keepdims=True), (tq, LANES))
    l1 = jnp.broadcast_to(jnp.sum(l_ref[1], axis=-1, keepdims=True), (tq, LANES))
    c1 = lam * l0 / l1
    m_ref[0] = c1

    a = (sc_ref[0] - _lanes(c1, t_ctx) * sc_ref[1]).astype(BF16)
    acc_ref[...] = jnp.dot(a, vc_ref[...], preferred_element_type=F32)

    if n_lat:
        def pass_c(c, carry):
            a = (sl_ref[0, c] - _lanes(m_ref[0], tk) * sl_ref[1, c]).astype(BF16)
            vc = vl_ref[pl.ds(pl.multiple_of(c * tk, tk), tk), :]
            acc_ref[...] += jnp.dot(a, vc, preferred_element_type=F32)
            return carry
        lax.fori_loop(0, n_lat, pass_c, 0)

    o = acc_ref[...] / l0
    ms = jnp.mean(o * o, axis=-1, keepdims=True)
    o = o * lax.rsqrt(ms + LN_EPS) * (1.0 - lam_init)
    o_ref[...] = (o * g_ref[...].astype(F32)).astype(BF16)


def _attention(p_q, p_ctx, p_lat, lam_rows, batch, q_len, ctx_len, lat_len, lam_init):
    m = batch * q_len
    tq = min(ATT_TQ, q_len)
    nq = q_len // tq
    tk = ATT_TK
    n_lat = 0 if p_lat is None else lat_len // tk
    width = 2 * ATT_DH

    def q_spec(g):
        return pl.BlockSpec((None, tq, width), lambda b, h, i: (g, b * nq + i, h))

    def kv_spec(g, length):
        return pl.BlockSpec((None, length, width), lambda b, h, i: (g, b, h))

    in_specs = [q_spec(G_AQ), kv_spec(G_AK, ctx_len), kv_spec(G_AV, ctx_len)]
    args = [p_q, p_ctx, p_ctx]
    if n_lat:
        in_specs += [kv_spec(G_AK, lat_len), kv_spec(G_AV, lat_len)]
        args += [p_lat, p_lat]
    in_specs += [q_spec(G_AG), pl.BlockSpec(lam_rows.shape, lambda b, h, i: (0, 0))]
    args += [p_q, lam_rows]
    scratch = [pltpu.VMEM((2, tq, ctx_len), F32)]
    if n_lat:
        scratch.append(pltpu.VMEM((2, n_lat, tq, tk), F32))
    scratch += [pltpu.VMEM((2, tq, LANES), F32), pltpu.VMEM((2, tq, LANES), F32), pltpu.VMEM((tq, width), F32)]
    return pl.pallas_call(
        functools.partial(_att_kernel, n_lat=n_lat, tk=tk, lam_init=lam_init),
        grid=(batch, ATT_HEADS, nq),
        in_specs=in_specs,
        out_specs=pl.BlockSpec((tq, width), lambda b, h, i: (b * nq + i, h)),
        out_shape=jax.ShapeDtypeStruct((m, ATT_HEADS * width), BF16),
        scratch_shapes=scratch,
        compiler_params=_params("arbitrary", "arbitrary", "arbitrary"),
        name="diff_attention_%d" % q_len,
    )(*args)


def _merge_kernel(yr_ref, ya_ref, cb_ref, cc_ref, cx_ref, cg_ref, ccp_ref, cxp_ref, ccn_ref, cxn_ref,
                  gr_ref, gc_ref, ga_ref, x_ref, gate_ref, cw_ref, lng_ref, lnb_ref,
                  wr_ref, wc_ref, wa_ref, wo_ref, o_ref, *, tiles_per_seq):
    i = pl.program_id(0)
    tm = x_ref.shape[0]
    t = i % tiles_per_seq
    u = cc_ref[...].astype(F32) * cx_ref[...].astype(F32)
    halo = BF16_SUBLANES
    u_before = ccp_ref[halo - 1:halo, :].astype(F32) * cxp_ref[halo - 1:halo, :].astype(F32)
    u_after = ccn_ref[0:1, :].astype(F32) * cxn_ref[0:1, :].astype(F32)
    u_before = jnp.where(t == 0, 0.0, u_before)
    u_after = jnp.where(t == tiles_per_seq - 1, 0.0, u_after)
    row = lax.broadcasted_iota(jnp.int32, u.shape, 0)
    u_prev = jnp.where(row == 0, u_before, pltpu.roll(u, 1, axis=0))
    u_next = jnp.where(row == tm - 1, u_after, pltpu.roll(u, tm - 1, axis=0))
    cw = cw_ref[...]
    conv = cw[0:1, :] * u_prev + cw[1:2, :] * u + cw[2:3, :] * u_next
    y_conv = (cb_ref[...].astype(F32) * conv * cg_ref[...].astype(F32)).astype(BF16)

    mix = gr_ref[...].astype(F32) * jnp.dot(yr_ref[...], wr_ref[...], preferred_element_type=F32)
    mix = mix + gc_ref[...].astype(F32) * jnp.dot(y_conv, wc_ref[...], preferred_element_type=F32)
    mix = mix + ga_ref[...].astype(F32) * jnp.dot(ya_ref[...], wa_ref[...], preferred_element_type=F32)
    out = jnp.dot(mix.astype(BF16), wo_ref[...], preferred_element_type=F32)
    z = DEEPNORM_ALPHA * x_ref[...] + gate_ref[...] * out
    o_ref[...] = _layer_norm_rows(z) * lng_ref[...] + lnb_ref[...]


def _merge(p, y_ret, y_att, x2d, mod_rows, mod_row_of_tile, conv_w, ln_g, ln_b, w_r, w_c, w_a, w_o, seq_len):
    m, d = x2d.shape
    tm = min(MERGE_TM, seq_len)
    tiles_per_seq = seq_len // tm
    halo = BF16_SUBLANES
    hb = tm // halo
    n_halo = m // halo

    def group_spec(g):
        return pl.BlockSpec((None, tm, d), lambda i: (g, i, 0))

    def before_spec(g):
        return pl.BlockSpec((None, halo, d), lambda i: (g, jnp.maximum(i * hb - 1, 0), 0))

    def after_spec(g):
        return pl.BlockSpec((None, halo, d), lambda i: (g, jnp.minimum((i + 1) * hb, n_halo - 1), 0))

    row_spec = pl.BlockSpec((tm, d), lambda i: (i, 0))
    vec_spec = pl.BlockSpec((1, d), lambda i: (0, 0))
    w_spec = pl.BlockSpec((d, d), lambda i: (0, 0))
    in_specs = [row_spec, row_spec,
                group_spec(G_CB), group_spec(G_CC), group_spec(G_CX), group_spec(G_CG),
                before_spec(G_CC), before_spec(G_CX), after_spec(G_CC), after_spec(G_CX),
                group_spec(G_MR), group_spec(G_MC), group_spec(G_MA),
                row_spec,
                pl.BlockSpec((None, 1, d), lambda i: (mod_row_of_tile(i), 0, 2)),
                pl.BlockSpec((3, d), lambda i: (0, 0)), vec_spec, vec_spec,
                w_spec, w_spec, w_spec, w_spec]
    args = [y_ret, y_att, p, p, p, p, p, p, p, p, p, p, p, x2d, mod_rows, conv_w,
            ln_g.reshape(1, d), ln_b.reshape(1, d), w_r, w_c, w_a, w_o]
    return pl.pallas_call(
        functools.partial(_merge_kernel, tiles_per_seq=tiles_per_seq),
        grid=(m // tm,),
        in_specs=in_specs,
        out_specs=row_spec,
        out_shape=jax.ShapeDtypeStruct((m, d), F32),
        compiler_params=_params("arbitrary"),
        name="merge_%d" % seq_len,
    )(*args)


def _rope_tables(seq_len):
    pos = jnp.arange(seq_len, dtype=jnp.int32)
    rows = (pos // GRID_W).astype(F32)
    cols = (pos % GRID_W).astype(F32)

    def angles(head_dim):
        n_freq = head_dim // 4
        inv = ROPE_BASE ** (-jnp.arange(n_freq, dtype=F32) / n_freq)
        return jnp.concatenate([rows[:, None] * inv, cols[:, None] * inv], axis=-1)

    ang_r = angles(RET_DK)
    ang_a = angles(ATT_DH)
    cos_a, sin_a = jnp.cos(ang_a), jnp.sin(ang_a)
    cos_att = jnp.concatenate([cos_a] * 4, axis=-1)
    sin_att = jnp.concatenate([-sin_a, sin_a, -sin_a, sin_a], axis=-1)
    return jnp.cos(ang_r), jnp.sin(ang_r), cos_att, sin_att


def kernel(x, c, ctx, c_ctx, w_mod, b_mod, w_in, ret_decay, conv_w, diff_lambda,
           w_ret_out, w_conv_out, w_att_out, w_out, ln_g, ln_b):
    bsz, t_lat, d = x.shape
    t_ctx = ctx.shape[1]
    depth = w_mod.shape[0]
    ctx_row = bsz

    c_rows = jnp.concatenate([c, c_ctx[None, :], jnp.zeros((8 - bsz - 1, d), F32)], axis=0)
    mod = _modulation(c_rows, w_mod, b_mod)
    rope_tabs = _rope_tables(t_lat)
    w_in_bf = w_in.astype(BF16)
    w_r, w_c, w_a, w_o = (w.astype(BF16) for w in (w_ret_out, w_conv_out, w_att_out, w_out))
    decay = jnp.broadcast_to(ret_decay.astype(F32).reshape(depth, 2 * RET_HEADS, 1, 1),
                             (depth, 2 * RET_HEADS, 1, LANES))

    lat_tiles_proj = t_lat // min(PROJ_TM, t_lat)
    lat_tiles_merge = t_lat // min(MERGE_TM, t_lat)
    x2d = x.reshape(bsz * t_lat, d)
    xc2d = ctx.reshape(bsz * t_ctx, d)
    for l in range(depth):
        last = l == depth - 1
        lam_init = 0.8 - 0.6 * math.exp(-0.3 * l)
        mod_rows = mod[l].reshape(8, 1, 3 * d)
        lam_rows = diff_lambda[l].astype(F32)

        n_ctx_groups = N_KV_GROUPS if last else N_GROUPS
        pc = _projection(xc2d, mod_rows, lambda i: ctx_row, w_in_bf[l], n_ctx_groups, t_ctx, None)
        cy_ret, states = _retention(pc, decay[l], bsz, t_ctx, None, emit_out=not last, emit_state=True)

        p = _projection(x2d, mod_rows, lambda i: i // lat_tiles_proj, w_in_bf[l], N_GROUPS, t_lat, rope_tabs)
        y_ret, _ = _retention(p, decay[l], bsz, t_lat, states, emit_out=True, emit_state=False)
        y_att = _attention(p, pc, p, lam_rows, bsz, t_lat, t_ctx, t_lat, lam_init)
        x_new = _merge(p, y_ret, y_att, x2d, mod_rows, lambda i: i // lat_tiles_merge, conv_w[l],
                       ln_g[l], ln_b[l], w_r[l], w_c[l], w_a[l], w_o[l], t_lat)
        if not last:
            cy_att = _attention(pc, pc, None, lam_rows, bsz, t_ctx, t_ctx, 0, lam_init)
            xc2d = _merge(pc, cy_ret, cy_att, xc2d, mod_rows, lambda i: ctx_row, conv_w[l],
                          ln_g[l], ln_b[l], w_r[l], w_c[l], w_a[l], w_o[l], t_ctx)
        x2d = x_new
    return x2d.reshape(bsz, t_lat, d)
```

```python
import functools
import math

import jax
import jax.numpy as jnp
from jax import lax
from jax.experimental import pallas as pl
from jax.experimental.pallas import tpu as pltpu

F32 = jnp.float32
BF16 = jnp.bfloat16

D_MODEL = 1024
DEPTH = 4
GRID_W = 64
RET_HEADS = 4
RET_DK = 256
RET_DV = 256
ATT_HEADS = 8
ATT_DH = 64
ROPE_BASE = 10000.0
LN_EPS = 1e-6
DEEPNORM_ALPHA = (2 * DEPTH) ** 0.25
LOG2_E = math.log2(math.e)

G_RK, G_RV, G_AK, G_AV, G_RQ, G_RG, G_AQ, G_AG, G_CB, G_CC, G_CX, G_CG, G_MR, G_MC, G_MA = range(15)
N_GROUPS = 15
N_KV_GROUPS = 4

LANES = 128
BF16_SUBLANES = 16
VMEM_LIMIT = 56 * 1024 * 1024

RET_CHUNK = 256
PROJ_TM = 1024
ATT_TQ = 512
ATT_TK = 512
MERGE_TM = 256


def _sigmoid(x):
    return 0.5 * jnp.tanh(0.5 * x) + 0.5


def _layer_norm_rows(x):
    mu = jnp.mean(x, axis=-1, keepdims=True)
    xc = x - mu
    var = jnp.mean(xc * xc, axis=-1, keepdims=True)
    return xc * lax.rsqrt(var + LN_EPS)


def _params(*sem):
    return pltpu.CompilerParams(dimension_semantics=sem, vmem_limit_bytes=VMEM_LIMIT)


def _mod_kernel(c_ref, w_ref, b_ref, o_ref):
    c = c_ref[...]
    s = (c * _sigmoid(c)).astype(BF16)
    o_ref[...] = jnp.dot(s, w_ref[...].astype(BF16), preferred_element_type=F32) + b_ref[...]


def _modulation(c_rows, w_mod, b_mod):
    depth, d, d3 = w_mod.shape
    tn = 1024
    return pl.pallas_call(
        _mod_kernel,
        grid=(depth, d3 // tn),
        in_specs=[
            pl.BlockSpec((8, d), lambda l, j: (0, 0)),
            pl.BlockSpec((None, d, tn), lambda l, j: (l, 0, j)),
            pl.BlockSpec((None, 1, tn), lambda l, j: (l, 0, j)),
        ],
        out_specs=pl.BlockSpec((None, 8, tn), lambda l, j: (l, 0, j)),
        out_shape=jax.ShapeDtypeStruct((depth, 8, d3), F32),
        compiler_params=_params("arbitrary", "arbitrary"),
        name="modulation",
    )(c_rows, w_mod, b_mod.reshape(depth, 1, d3))


def _proj_kernel(*refs, rope):
    if rope:
        x_ref, sh_ref, sc_ref, w_ref, cr_ref, sr_ref, ca_ref, sa_ref, o_ref, h_ref = refs
    else:
        x_ref, sh_ref, sc_ref, w_ref, o_ref, h_ref = refs
    j = pl.program_id(1)

    @pl.when(j == 0)
    def _():
        y = _layer_norm_rows(x_ref[...])
        h_ref[...] = (y * (1.0 + sc_ref[...]) + sh_ref[...]).astype(BF16)

    def product():
        return jnp.dot(h_ref[...], w_ref[...], preferred_element_type=F32)

    def store_rope_ret(scale):
        acc = product()
        c = cr_ref[...]
        s = sr_ref[...]
        for hh in range(RET_HEADS):
            lo = hh * RET_DK
            mid = lo + RET_DK // 2
            hi = lo + RET_DK
            x1 = acc[:, lo:mid]
            x2 = acc[:, mid:hi]
            o_ref[:, lo:mid] = ((x1 * c - x2 * s) * scale).astype(BF16)
            o_ref[:, mid:hi] = ((x1 * s + x2 * c) * scale).astype(BF16)

    def store_rope_att(scale):
        acc = product()
        c = ca_ref[...]
        s = sa_ref[...]
        lane = lax.broadcasted_iota(jnp.int32, c.shape, 1)
        first_half = (lane % ATT_DH) < (ATT_DH // 2)
        for hh in range(ATT_HEADS):
            lo = hh * LANES
            x = acc[:, lo:lo + LANES]
            partner = jnp.where(first_half,
                                pltpu.roll(x, LANES - ATT_DH // 2, axis=1),
                                pltpu.roll(x, ATT_DH // 2, axis=1))
            o_ref[:, lo:lo + LANES] = ((x * c + partner * s) * scale).astype(BF16)

    def store_scaled(scale):
        o_ref[...] = (product() * scale).astype(BF16)

    def store_plain():
        o_ref[...] = product().astype(BF16)

    def store_silu():
        a = product()
        o_ref[...] = (a * _sigmoid(a)).astype(BF16)

    def store_sigmoid():
        o_ref[...] = _sigmoid(product()).astype(BF16)

    q_r_scale = RET_DK ** -0.5
    q_a_scale = ATT_DH ** -0.5 * LOG2_E
    if rope:
        pl.when(j == G_RK)(functools.partial(store_rope_ret, 1.0))
        pl.when(j == G_RQ)(functools.partial(store_rope_ret, q_r_scale))
        pl.when(j == G_AK)(functools.partial(store_rope_att, 1.0))
        pl.when(j == G_AQ)(functools.partial(store_rope_att, q_a_scale))
        plain = (G_RV, G_AV, G_CB, G_CC, G_CX)
    else:
        pl.when(j == G_RQ)(functools.partial(store_scaled, q_r_scale))
        pl.when(j == G_AQ)(functools.partial(store_scaled, q_a_scale))
        plain = (G_RK, G_AK, G_RV, G_AV, G_CB, G_CC, G_CX)
    is_plain = functools.reduce(jnp.logical_or, [j == g for g in plain])
    pl.when(is_plain)(store_plain)
    is_silu = functools.reduce(jnp.logical_or, [j == g for g in (G_RG, G_AG, G_CG)])
    pl.when(is_silu)(store_silu)
    pl.when(j >= G_MR)(store_sigmoid)


def _projection(x2d, mod_rows, mod_row_of_tile, w_bf16, n_groups, seq_len, rope_tabs):
    m, d = x2d.shape
    tm = min(PROJ_TM, seq_len)
    rope = rope_tabs is not None
    tiles_per_seq = seq_len // tm
    in_specs = [
        pl.BlockSpec((tm, d), lambda i, j: (i, 0)),
        pl.BlockSpec((None, 1, d), lambda i, j: (mod_row_of_tile(i), 0, 0)),
        pl.BlockSpec((None, 1, d), lambda i, j: (mod_row_of_tile(i), 0, 1)),
        pl.BlockSpec((d, d), lambda i, j: (0, j)),
    ]
    args = [x2d, mod_rows, mod_rows, w_bf16]
    if rope:
        tab_spec = pl.BlockSpec((tm, LANES), lambda i, j: (i % tiles_per_seq, 0))
        in_specs += [tab_spec] * 4
        args += list(rope_tabs)
    return pl.pallas_call(
        functools.partial(_proj_kernel, rope=rope),
        grid=(m // tm, n_groups),
        in_specs=in_specs,
        out_specs=pl.BlockSpec((None, tm, d), lambda i, j: (j, i, 0)),
        out_shape=jax.ShapeDtypeStruct((n_groups, m, d), BF16),
        scratch_shapes=[pltpu.VMEM((tm, d), BF16)],
        compiler_params=_params("arbitrary", "arbitrary"),
        name="projection_rope" if rope else "projection_ctx",
    )(*args)


def _lanes(x, n):
    return x if n == LANES else jnp.concatenate([x] * (n // LANES), axis=1)


_KDEC_F, _KDEC_B, _QDEC_F, _QDEC_B = range(4)


def _ret_kernel(*refs, seq_len, chunk, zero_init, emit_out, emit_state):
    refs = list(refs)
    k_ref, v_ref = refs.pop(0), refs.pop(0)
    q_ref = g_ref = None
    if emit_out:
        q_ref, g_ref = refs.pop(0), refs.pop(0)
    df_ref, db_ref = refs.pop(0), refs.pop(0)
    sf0_ref = sb0_ref = None
    if not zero_init:
        sf0_ref, sb0_ref = refs.pop(0), refs.pop(0)
    y_ref = sfo_ref = sbo_ref = None
    if emit_out:
        y_ref = refs.pop(0)
    if emit_state:
        sfo_ref, sbo_ref = refs.pop(0), refs.pop(0)
    sf_ref, sb_ref, dec_ref = refs.pop(0), refs.pop(0), refs.pop(0)
    dmat_ref = bwd_ref = None
    if emit_out:
        dmat_ref, bwd_ref = refs.pop(0), refs.pop(0)

    n_chunks = seq_len // chunk
    contract_rows = (((0,), (0,)), ((), ()))
    contract_last = (((1,), (1,)), ((), ()))

    lg_f = _lanes(-jnp.exp(df_ref[...]), RET_DK)
    lg_b = _lanes(-jnp.exp(db_ref[...]), RET_DK)
    pos = lax.broadcasted_iota(jnp.int32, (chunk, RET_DK), 0).astype(F32)
    dec_ref[_KDEC_F] = jnp.exp(lg_f * (chunk - 1.0 - pos))
    dec_ref[_KDEC_B] = jnp.exp(lg_b * pos)
    gc_f = jnp.exp(lg_f * float(chunk))
    gc_b = jnp.exp(lg_b * float(chunk))
    if emit_out:
        dec_ref[_QDEC_F] = jnp.exp(lg_f * (pos + 1.0))
        dec_ref[_QDEC_B] = jnp.exp(lg_b * (float(chunk) - pos))
        row = lax.broadcasted_iota(jnp.int32, (chunk, chunk), 0)
        col = lax.broadcasted_iota(jnp.int32, (chunk, chunk), 1)
        dist = (row - col).astype(F32)
        lgf_c = _lanes(lg_f[:, :LANES], chunk)
        lgb_c = _lanes(lg_b[:, :LANES], chunk)
        dmat_ref[...] = jnp.where(dist >= 0.0,
                                  jnp.exp(lgf_c * jnp.maximum(dist, 0.0)),
                                  jnp.exp(lgb_c * jnp.maximum(-dist, 0.0)))

    if zero_init:
        sf_ref[...] = jnp.zeros_like(sf_ref)
        sb_ref[...] = jnp.zeros_like(sb_ref)
    else:
        sf_ref[...] = sf0_ref[...]
        sb_ref[...] = sb0_ref[...]

    def rows(n):
        return pl.ds(pl.multiple_of(n * chunk, chunk), chunk)

    def state_step(s_ref, kn, vn, kdec, gc):
        kd = (kn.astype(F32) * kdec).astype(BF16)
        s_ref[...] = gc * s_ref[...] + lax.dot_general(kd, vn, contract_rows, preferred_element_type=F32)

    def bwd_body(t, carry):
        n = n_chunks - 1 - t
        r = rows(n)
        kn, vn = k_ref[r, :], v_ref[r, :]
        if emit_out:
            cross = jnp.dot(q_ref[r, :], sb_ref[...].astype(BF16), preferred_element_type=F32)
            bwd_ref[r, :] = cross * dec_ref[_QDEC_B]
        state_step(sb_ref, kn, vn, dec_ref[_KDEC_B], gc_b)
        return carry

    lax.fori_loop(0, n_chunks, bwd_body, 0)

    def fwd_body(n, carry):
        r = rows(n)
        kn, vn = k_ref[r, :], v_ref[r, :]
        if emit_out:
            qn = q_ref[r, :]
            scores = lax.dot_general(qn, kn, contract_last, preferred_element_type=F32) * dmat_ref[...]
            o = jnp.dot(scores.astype(BF16), vn, preferred_element_type=F32)
            o = o + jnp.dot(qn, sf_ref[...].astype(BF16), preferred_element_type=F32) * dec_ref[_QDEC_F]
            o = o + bwd_ref[r, :]
            y_ref[r, :] = (_layer_norm_rows(o) * g_ref[r, :].astype(F32)).astype(BF16)
        state_step(sf_ref, kn, vn, dec_ref[_KDEC_F], gc_f)
        return carry

    lax.fori_loop(0, n_chunks, fwd_body, 0)

    if emit_state:
        sfo_ref[...] = sf_ref[...]
        sbo_ref[...] = sb_ref[...]


def _retention(p, decay, batch, seq_len, states, emit_out, emit_state):
    m = batch * seq_len
    chunk = min(RET_CHUNK, seq_len)

    def group_spec(g):
        return pl.BlockSpec((None, seq_len, RET_DK), lambda b, h: (g, b, h))

    state_spec = pl.BlockSpec((None, None, RET_DK, RET_DV), lambda b, h: (b, h, 0, 0))
    in_specs = [group_spec(G_RK), group_spec(G_RV)]
    args = [p, p]
    if emit_out:
        in_specs += [group_spec(G_RQ), group_spec(G_RG)]
        args += [p, p]
    in_specs += [pl.BlockSpec((None, 1, LANES), lambda b, h: (h, 0, 0)),
                 pl.BlockSpec((None, 1, LANES), lambda b, h: (RET_HEADS + h, 0, 0))]
    args += [decay, decay]
    if states is not None:
        in_specs += [state_spec, state_spec]
        args += list(states)
    out_specs, out_shape = [], []
    if emit_out:
        out_specs.append(pl.BlockSpec((seq_len, RET_DV), lambda b, h: (b, h)))
        out_shape.append(jax.ShapeDtypeStruct((m, RET_HEADS * RET_DV), BF16))
    if emit_state:
        out_specs += [state_spec, state_spec]
        out_shape += [jax.ShapeDtypeStruct((batch, RET_HEADS, RET_DK, RET_DV), F32)] * 2
    scratch = [pltpu.VMEM((RET_DK, RET_DV), F32), pltpu.VMEM((RET_DK, RET_DV), F32),
               pltpu.VMEM((4, chunk, RET_DK), F32)]
    if emit_out:
        scratch += [pltpu.VMEM((chunk, chunk), F32), pltpu.VMEM((seq_len, RET_DV), F32)]
    outs = pl.pallas_call(
        functools.partial(_ret_kernel, seq_len=seq_len, chunk=chunk, zero_init=states is None,
                          emit_out=emit_out, emit_state=emit_state),
        grid=(batch, RET_HEADS),
        in_specs=in_specs,
        out_specs=out_specs,
        out_shape=out_shape,
        scratch_shapes=scratch,
        compiler_params=_params("arbitrary", "arbitrary"),
        name="retention_%d" % seq_len,
    )(*args)
    outs = list(outs)
    y = outs.pop(0) if emit_out else None
    st = tuple(outs) if emit_state else None
    return y, st


def _att_kernel(*refs, n_lat, tk, lam_init):
    refs = list(refs)
    q_ref, kc_ref, vc_ref = refs.pop(0), refs.pop(0), refs.pop(0)
    kl_ref = vl_ref = None
    if n_lat:
        kl_ref, vl_ref = refs.pop(0), refs.pop(0)
    g_ref, lam_ref, o_ref = refs.pop(0), refs.pop(0), refs.pop(0)
    pc_ref = refs.pop(0)
    pl_ref = refs.pop(0) if n_lat else None
    mc_ref, lc_ref, coef_ref, acc_ref = refs

    tq = q_ref.shape[0]
    t_ctx = kc_ref.shape[0]
    n_chunks = 1 + n_lat
    contract_last = (((1,), (1,)), ((), ()))
    q = q_ref[...]
    lane = lax.broadcasted_iota(jnp.int32, q.shape, 1)
    zero = jnp.zeros_like(q)
    qm = (jnp.where(lane < ATT_DH, q, zero), jnp.where(lane >= ATT_DH, q, zero))

    def lane_groups(x):
        return [x[:, g * LANES:(g + 1) * LANES] for g in range(x.shape[1] // LANES)]

    def group_max(x):
        return functools.reduce(jnp.maximum, lane_groups(x))

    def group_sum(x):
        return functools.reduce(jnp.add, lane_groups(x))

    def keys(c):
        return kc_ref[...] if c == 0 else kl_ref[(c - 1) * tk:c * tk, :]

    def values(c):
        return vc_ref[...] if c == 0 else vl_ref[(c - 1) * tk:c * tk, :]

    def probs(j, c):
        return pc_ref.at[j] if c == 0 else pl_ref.at[j, c - 1]

    for c in range(n_chunks):
        k = keys(c)
        for j in range(2):
            s = lax.dot_general(qm[j], k, contract_last, preferred_element_type=F32)
            mb = jnp.broadcast_to(jnp.max(group_max(s), axis=-1, keepdims=True), (tq, LANES))
            p = jnp.exp2(s - _lanes(mb, s.shape[1]))
            mc_ref[j, c] = mb
            lc_ref[j, c] = group_sum(p)
            probs(j, c)[...] = p.astype(BF16)

    denom = []
    for j in range(2):
        m = functools.reduce(jnp.maximum, [mc_ref[j, c] for c in range(n_chunks)])
        part = None
        for c in range(n_chunks):
            w = jnp.exp2(mc_ref[j, c] - m)
            mc_ref[j, c] = w
            part = w * lc_ref[j, c] if part is None else part + w * lc_ref[j, c]
        denom.append(jnp.broadcast_to(jnp.sum(part, axis=-1, keepdims=True), (tq, LANES)))
    l0, l1 = denom

    lv = lam_ref[...]
    lam = (jnp.exp(jnp.sum(lv[0:1, :] * lv[1:2, :], axis=-1, keepdims=True))
           - jnp.exp(jnp.sum(lv[2:3, :] * lv[3:4, :], axis=-1, keepdims=True)) + lam_init)
    c1 = lam * l0 / l1
    for c in range(n_chunks):
        coef_ref[0, c] = mc_ref[0, c].astype(BF16)
        coef_ref[1, c] = (-c1 * mc_ref[1, c]).astype(BF16)

    for c in range(n_chunks):
        width = t_ctx if c == 0 else tk
        a = probs(0, c)[...] * _lanes(coef_ref[0, c], width) + probs(1, c)[...] * _lanes(coef_ref[1, c], width)
        pv = jnp.dot(a, values(c), preferred_element_type=F32)
        if c == 0:
            acc_ref[...] = pv
        else:
            acc_ref[...] += pv

    o = acc_ref[...] / l0
    ms = jnp.mean(o * o, axis=-1, keepdims=True)
    o = o * lax.rsqrt(ms + LN_EPS) * (1.0 - lam_init)
    o_ref[...] = (o * g_ref[...].astype(F32)).astype(BF16)


def _attention(p_q, p_ctx, p_lat, lam_rows, batch, q_len, ctx_len, lat_len, lam_init):
    m = batch * q_len
    tq = min(ATT_TQ, q_len)
    nq = q_len // tq
    tk = ATT_TK
    n_lat = 0 if p_lat is None else lat_len // tk
    width = 2 * ATT_DH

    def q_spec(g):
        return pl.BlockSpec((None, tq, width), lambda b, h, i: (g, b * nq + i, h))

    def kv_spec(g, length):
        return pl.BlockSpec((None, length, width), lambda b, h, i: (g, b, h))

    in_specs = [q_spec(G_AQ), kv_spec(G_AK, ctx_len), kv_spec(G_AV, ctx_len)]
    args = [p_q, p_ctx, p_ctx]
    if n_lat:
        in_specs += [kv_spec(G_AK, lat_len), kv_spec(G_AV, lat_len)]
        args += [p_lat, p_lat]
    in_specs += [q_spec(G_AG), pl.BlockSpec(lam_rows.shape, lambda b, h, i: (0, 0))]
    args += [p_q, lam_rows]
    n_chunks = 1 + n_lat
    scratch = [pltpu.VMEM((2, tq, ctx_len), BF16)]
    if n_lat:
        scratch.append(pltpu.VMEM((2, n_lat, tq, tk), BF16))
    scratch += [pltpu.VMEM((2, n_chunks, tq, LANES), F32), pltpu.VMEM((2, n_chunks, tq, LANES), F32),
                pltpu.VMEM((2, n_chunks, tq, LANES), BF16), pltpu.VMEM((tq, width), F32)]
    return pl.pallas_call(
        functools.partial(_att_kernel, n_lat=n_lat, tk=tk, lam_init=lam_init),
        grid=(batch, ATT_HEADS, nq),
        in_specs=in_specs,
        out_specs=pl.BlockSpec((tq, width), lambda b, h, i: (b * nq + i, h)),
        out_shape=jax.ShapeDtypeStruct((m, ATT_HEADS * width), BF16),
        scratch_shapes=scratch,
        compiler_params=_params("arbitrary", "arbitrary", "arbitrary"),
        name="diff_attention_%d" % q_len,
    )(*args)


def _merge_kernel(yr_ref, ya_ref, cb_ref, cc_ref, cx_ref, cg_ref, ccp_ref, cxp_ref, ccn_ref, cxn_ref,
                  gr_ref, gc_ref, ga_ref, x_ref, gate_ref, cw_ref, lng_ref, lnb_ref,
                  wr_ref, wc_ref, wa_ref, wo_ref, o_ref, *, tiles_per_seq):
    i = pl.program_id(0)
    tm = x_ref.shape[0]
    t = i % tiles_per_seq
    u = cc_ref[...].astype(F32) * cx_ref[...].astype(F32)
    halo = BF16_SUBLANES
    u_before = ccp_ref[halo - 1:halo, :].astype(F32) * cxp_ref[halo - 1:halo, :].astype(F32)
    u_after = ccn_ref[0:1, :].astype(F32) * cxn_ref[0:1, :].astype(F32)
    u_before = jnp.where(t == 0, 0.0, u_before)
    u_after = jnp.where(t == tiles_per_seq - 1, 0.0, u_after)
    row = lax.broadcasted_iota(jnp.int32, u.shape, 0)
    u_prev = jnp.where(row == 0, u_before, pltpu.roll(u, 1, axis=0))
    u_next = jnp.where(row == tm - 1, u_after, pltpu.roll(u, tm - 1, axis=0))
    cw = cw_ref[...]
    conv = cw[0:1, :] * u_prev + cw[1:2, :] * u + cw[2:3, :] * u_next
    y_conv = (cb_ref[...].astype(F32) * conv * cg_ref[...].astype(F32)).astype(BF16)

    mix = gr_ref[...].astype(F32) * jnp.dot(yr_ref[...], wr_ref[...], preferred_element_type=F32)
    mix = mix + gc_ref[...].astype(F32) * jnp.dot(y_conv, wc_ref[...], preferred_element_type=F32)
    mix = mix + ga_ref[...].astype(F32) * jnp.dot(ya_ref[...], wa_ref[...], preferred_element_type=F32)
    out = jnp.dot(mix.astype(BF16), wo_ref[...], preferred_element_type=F32)
    z = DEEPNORM_ALPHA * x_ref[...] + gate_ref[...] * out
    o_ref[...] = _layer_norm_rows(z) * lng_ref[...] + lnb_ref[...]


def _merge(p, y_ret, y_att, x2d, mod_rows, mod_row_of_tile, conv_w, ln_g, ln_b, w_r, w_c, w_a, w_o, seq_len):
    m, d = x2d.shape
    tm = min(MERGE_TM, seq_len)
    tiles_per_seq = seq_len // tm
    halo = BF16_SUBLANES
    hb = tm // halo
    n_halo = m // halo

    def group_spec(g):
        return pl.BlockSpec((None, tm, d), lambda i: (g, i, 0))

    def before_spec(g):
        return pl.BlockSpec((None, halo, d), lambda i: (g, jnp.maximum(i * hb - 1, 0), 0))

    def after_spec(g):
        return pl.BlockSpec((None, halo, d), lambda i: (g, jnp.minimum((i + 1) * hb, n_halo - 1), 0))

    row_spec = pl.BlockSpec((tm, d), lambda i: (i, 0))
    vec_spec = pl.BlockSpec((1, d), lambda i: (0, 0))
    w_spec = pl.BlockSpec((d, d), lambda i: (0, 0))
    in_specs = [row_spec, row_spec,
                group_spec(G_CB), group_spec(G_CC), group_spec(G_CX), group_spec(G_CG),
                before_spec(G_CC), before_spec(G_CX), after_spec(G_CC), after_spec(G_CX),
                group_spec(G_MR), group_spec(G_MC), group_spec(G_MA),
                row_spec,
                pl.BlockSpec((None, 1, d), lambda i: (mod_row_of_tile(i), 0, 2)),
                pl.BlockSpec((3, d), lambda i: (0, 0)), vec_spec, vec_spec,
                w_spec, w_spec, w_spec, w_spec]
    args = [y_ret, y_att, p, p, p, p, p, p, p, p, p, p, p, x2d, mod_rows, conv_w,
            ln_g.reshape(1, d), ln_b.reshape(1, d), w_r, w_c, w_a, w_o]
    return pl.pallas_call(
        functools.partial(_merge_kernel, tiles_per_seq=tiles_per_seq),
        grid=(m // tm,),
        in_specs=in_specs,
        out_specs=row_spec,
        out_shape=jax.ShapeDtypeStruct((m, d), F32),
        compiler_params=_params("arbitrary"),
        name="merge_%d" % seq_len,
    )(*args)


def _rope_tables(seq_len):
    pos = jnp.arange(seq_len, dtype=jnp.int32)
    rows = (pos // GRID_W).astype(F32)
    cols = (pos % GRID_W).astype(F32)

    def angles(head_dim):
        n_freq = head_dim // 4
        inv = ROPE_BASE ** (-jnp.arange(n_freq, dtype=F32) / n_freq)
        return jnp.concatenate([rows[:, None] * inv, cols[:, None] * inv], axis=-1)

    ang_r = angles(RET_DK)
    ang_a = angles(ATT_DH)
    cos_a, sin_a = jnp.cos(ang_a), jnp.sin(ang_a)
    cos_att = jnp.concatenate([cos_a] * 4, axis=-1)
    sin_att = jnp.concatenate([-sin_a, sin_a, -sin_a, sin_a], axis=-1)
    return jnp.cos(ang_r), jnp.sin(ang_r), cos_att, sin_att


def kernel(x, c, ctx, c_ctx, w_mod, b_mod, w_in, ret_decay, conv_w, diff_lambda,
           w_ret_out, w_conv_out, w_att_out, w_out, ln_g, ln_b):
    bsz, t_lat, d = x.shape
    t_ctx = ctx.shape[1]
    depth = w_mod.shape[0]
    ctx_row = bsz

    c_rows = jnp.concatenate([c, c_ctx[None, :], jnp.zeros((8 - bsz - 1, d), F32)], axis=0)
    mod = _modulation(c_rows, w_mod, b_mod)
    rope_tabs = _rope_tables(t_lat)
    w_in_bf = w_in.astype(BF16)
    w_r, w_c, w_a, w_o = (w.astype(BF16) for w in (w_ret_out, w_conv_out, w_att_out, w_out))
    decay = jnp.broadcast_to(ret_decay.astype(F32).reshape(depth, 2 * RET_HEADS, 1, 1),
                             (depth, 2 * RET_HEADS, 1, LANES))

    lat_tiles_proj = t_lat // min(PROJ_TM, t_lat)
    lat_tiles_merge = t_lat // min(MERGE_TM, t_lat)
    x2d = x.reshape(bsz * t_lat, d)
    xc2d = ctx.reshape(bsz * t_ctx, d)
    for l in range(depth):
        last = l == depth - 1
        lam_init = 0.8 - 0.6 * math.exp(-0.3 * l)
        mod_rows = mod[l].reshape(8, 1, 3 * d)
        lam_rows = diff_lambda[l].astype(F32)

        n_ctx_groups = N_KV_GROUPS if last else N_GROUPS
        pc = _projection(xc2d, mod_rows, lambda i: ctx_row, w_in_bf[l], n_ctx_groups, t_ctx, None)
        cy_ret, states = _retention(pc, decay[l], bsz, t_ctx, None, emit_out=not last, emit_state=True)

        p = _projection(x2d, mod_rows, lambda i: i // lat_tiles_proj, w_in_bf[l], N_GROUPS, t_lat, rope_tabs)
        y_ret, _ = _retention(p, decay[l], bsz, t_lat, states, emit_out=True, emit_state=False)
        y_att = _attention(p, pc, p, lam_rows, bsz, t_lat, t_ctx, t_lat, lam_init)
        x_new = _merge(p, y_ret, y_att, x2d, mod_rows, lambda i: i // lat_tiles_merge, conv_w[l],
                       ln_g[l], ln_b[l], w_r[l], w_c[l], w_a[l], w_o[l], t_lat)
        if not last:
            cy_att = _attention(pc, pc, None, lam_rows, bsz, t_ctx, t_ctx, 0, lam_init)
            xc2d = _merge(pc, cy_ret, cy_att, xc2d, mod_rows, lambda i: ctx_row, conv_w[l],
                          ln_g[l], ln_b[l], w_r[l], w_c[l], w_a[l], w_o[l], t_ctx)
        x2d = x_new
    return x2d.reshape(bsz, t_lat, d)
```

```python
import functools
import math

import jax
import jax.numpy as jnp
from jax import lax
from jax.experimental import pallas as pl
from jax.experimental.pallas import tpu as pltpu

F32 = jnp.float32
BF16 = jnp.bfloat16

D_MODEL = 1024
DEPTH = 4
GRID_W = 64
RET_HEADS = 4
RET_DK = 256
RET_DV = 256
ATT_HEADS = 8
ATT_DH = 64
ROPE_BASE = 10000.0
LN_EPS = 1e-6
DEEPNORM_ALPHA = (2 * DEPTH) ** 0.25
LOG2_E = math.log2(math.e)

G_RK, G_RV, G_AK, G_AV, G_RQ, G_RG, G_AQ, G_AG, G_CB, G_CC, G_CX, G_CG, G_MR, G_MC, G_MA = range(15)
N_GROUPS = 15
N_KV_GROUPS = 4

LANES = 128
BF16_SUBLANES = 16
VMEM_LIMIT = 56 * 1024 * 1024

RET_CHUNK = 256
PROJ_TM = 1024
ATT_TQ = 512
ATT_TK = 512
MERGE_TM = 256


def _sigmoid(x):
    return 0.5 * jnp.tanh(0.5 * x) + 0.5


def _layer_norm_rows(x):
    mu = jnp.mean(x, axis=-1, keepdims=True)
    xc = x - mu
    var = jnp.mean(xc * xc, axis=-1, keepdims=True)
    return xc * lax.rsqrt(var + LN_EPS)


def _params(*sem):
    return pltpu.CompilerParams(dimension_semantics=sem, vmem_limit_bytes=VMEM_LIMIT)


def _mod_kernel(c_ref, w_ref, b_ref, o_ref):
    c = c_ref[...]
    s = (c * _sigmoid(c)).astype(BF16)
    o_ref[...] = jnp.dot(s, w_ref[...].astype(BF16), preferred_element_type=F32) + b_ref[...]


def _modulation(c_rows, w_mod, b_mod):
    depth, d, d3 = w_mod.shape
    tn = 1024
    return pl.pallas_call(
        _mod_kernel,
        grid=(depth, d3 // tn),
        in_specs=[
            pl.BlockSpec((8, d), lambda l, j: (0, 0)),
            pl.BlockSpec((None, d, tn), lambda l, j: (l, 0, j)),
            pl.BlockSpec((None, 1, tn), lambda l, j: (l, 0, j)),
        ],
        out_specs=pl.BlockSpec((None, 8, tn), lambda l, j: (l, 0, j)),
        out_shape=jax.ShapeDtypeStruct((depth, 8, d3), F32),
        compiler_params=_params("arbitrary", "arbitrary"),
        name="modulation",
    )(c_rows, w_mod, b_mod.reshape(depth, 1, d3))


def _proj_kernel(*refs, rope):
    if rope:
        x_ref, sh_ref, sc_ref, w_ref, cr_ref, sr_ref, ca_ref, sa_ref, o_ref, h_ref = refs
    else:
        x_ref, sh_ref, sc_ref, w_ref, o_ref, h_ref = refs
    j = pl.program_id(1)

    @pl.when(j == 0)
    def _():
        y = _layer_norm_rows(x_ref[...])
        h_ref[...] = (y * (1.0 + sc_ref[...]) + sh_ref[...]).astype(BF16)

    def product():
        return jnp.dot(h_ref[...], w_ref[...], preferred_element_type=F32)

    def store_rope_ret(scale):
        acc = product()
        c = cr_ref[...]
        s = sr_ref[...]
        for hh in range(RET_HEADS):
            lo = hh * RET_DK
            mid = lo + RET_DK // 2
            hi = lo + RET_DK
            x1 = acc[:, lo:mid]
            x2 = acc[:, mid:hi]
            o_ref[:, lo:mid] = ((x1 * c - x2 * s) * scale).astype(BF16)
            o_ref[:, mid:hi] = ((x1 * s + x2 * c) * scale).astype(BF16)

    def store_rope_att(scale):
        acc = product()
        c = ca_ref[...]
        s = sa_ref[...]
        lane = lax.broadcasted_iota(jnp.int32, c.shape, 1)
        first_half = (lane % ATT_DH) < (ATT_DH // 2)
        for hh in range(ATT_HEADS):
            lo = hh * LANES
            x = acc[:, lo:lo + LANES]
            partner = jnp.where(first_half,
                                pltpu.roll(x, LANES - ATT_DH // 2, axis=1),
                                pltpu.roll(x, ATT_DH // 2, axis=1))
            o_ref[:, lo:lo + LANES] = ((x * c + partner * s) * scale).astype(BF16)

    def store_scaled(scale):
        o_ref[...] = (product() * scale).astype(BF16)

    def store_plain():
        o_ref[...] = product().astype(BF16)

    def store_silu():
        a = product()
        o_ref[...] = (a * _sigmoid(a)).astype(BF16)

    def store_sigmoid():
        o_ref[...] = _sigmoid(product()).astype(BF16)

    q_r_scale = RET_DK ** -0.5
    q_a_scale = ATT_DH ** -0.5 * LOG2_E
    if rope:
        pl.when(j == G_RK)(functools.partial(store_rope_ret, 1.0))
        pl.when(j == G_RQ)(functools.partial(store_rope_ret, q_r_scale))
        pl.when(j == G_AK)(functools.partial(store_rope_att, 1.0))
        pl.when(j == G_AQ)(functools.partial(store_rope_att, q_a_scale))
        plain = (G_RV, G_AV, G_CB, G_CC, G_CX)
    else:
        pl.when(j == G_RQ)(functools.partial(store_scaled, q_r_scale))
        pl.when(j == G_AQ)(functools.partial(store_scaled, q_a_scale))
        plain = (G_RK, G_AK, G_RV, G_AV, G_CB, G_CC, G_CX)
    is_plain = functools.reduce(jnp.logical_or, [j == g for g in plain])
    pl.when(is_plain)(store_plain)
    is_silu = functools.reduce(jnp.logical_or, [j == g for g in (G_RG, G_AG, G_CG)])
    pl.when(is_silu)(store_silu)
    pl.when(j >= G_MR)(store_sigmoid)


def _projection(x2d, mod_rows, mod_row_of_tile, w_bf16, layer, n_groups, seq_len, rope_tabs):
    m, d = x2d.shape
    tm = min(PROJ_TM, seq_len)
    rope = rope_tabs is not None
    tiles_per_seq = seq_len // tm
    in_specs = [
        pl.BlockSpec((tm, d), lambda i, j: (i, 0)),
        pl.BlockSpec((None, 1, d), lambda i, j: (mod_row_of_tile(i), 0, 0)),
        pl.BlockSpec((None, 1, d), lambda i, j: (mod_row_of_tile(i), 0, 1)),
        pl.BlockSpec((None, d, d), lambda i, j: (layer, 0, j)),
    ]
    args = [x2d, mod_rows, mod_rows, w_bf16]
    if rope:
        tab_spec = pl.BlockSpec((tm, LANES), lambda i, j: (i % tiles_per_seq, 0))
        in_specs += [tab_spec] * 4
        args += list(rope_tabs)
    return pl.pallas_call(
        functools.partial(_proj_kernel, rope=rope),
        grid=(m // tm, n_groups),
        in_specs=in_specs,
        out_specs=pl.BlockSpec((None, tm, d), lambda i, j: (j, i, 0)),
        out_shape=jax.ShapeDtypeStruct((n_groups, m, d), BF16),
        scratch_shapes=[pltpu.VMEM((tm, d), BF16)],
        compiler_params=_params("arbitrary", "arbitrary"),
        name="projection_rope" if rope else "projection_ctx",
    )(*args)


def _lanes(x, n):
    return x if n == LANES else jnp.concatenate([x] * (n // LANES), axis=1)


_KDEC_F, _KDEC_B, _QDEC_F, _QDEC_B = range(4)


def _ret_kernel(*refs, seq_len, chunk, zero_init, emit_out, emit_state):
    refs = list(refs)
    k_ref, v_ref = refs.pop(0), refs.pop(0)
    q_ref = g_ref = None
    if emit_out:
        q_ref, g_ref = refs.pop(0), refs.pop(0)
    df_ref, db_ref = refs.pop(0), refs.pop(0)
    sf0_ref = sb0_ref = None
    if not zero_init:
        sf0_ref, sb0_ref = refs.pop(0), refs.pop(0)
    y_ref = sfo_ref = sbo_ref = None
    if emit_out:
        y_ref = refs.pop(0)
    if emit_state:
        sfo_ref, sbo_ref = refs.pop(0), refs.pop(0)
    sf_ref, sb_ref, dec_ref, incf_ref, incb_ref = (refs.pop(0) for _ in range(5))
    dmat_ref = sfs_ref = sbs_ref = None
    if emit_out:
        dmat_ref, sfs_ref, sbs_ref = refs.pop(0), refs.pop(0), refs.pop(0)

    n_chunks = seq_len // chunk
    contract_rows = (((0,), (0,)), ((), ()))
    contract_last = (((1,), (1,)), ((), ()))

    lg_f = _lanes(-jnp.exp(df_ref[...]), RET_DK)
    lg_b = _lanes(-jnp.exp(db_ref[...]), RET_DK)
    pos = lax.broadcasted_iota(jnp.int32, (chunk, RET_DK), 0).astype(F32)
    dec_ref[_KDEC_F] = jnp.exp(lg_f * (chunk - 1.0 - pos))
    dec_ref[_KDEC_B] = jnp.exp(lg_b * pos)
    gc_f = jnp.exp(lg_f * float(chunk))
    gc_b = jnp.exp(lg_b * float(chunk))
    if emit_out:
        dec_ref[_QDEC_F] = jnp.exp(lg_f * (pos + 1.0))
        dec_ref[_QDEC_B] = jnp.exp(lg_b * (float(chunk) - pos))
        row = lax.broadcasted_iota(jnp.int32, (chunk, chunk), 0)
        col = lax.broadcasted_iota(jnp.int32, (chunk, chunk), 1)
        dist = (row - col).astype(F32)
        lgf_c = _lanes(lg_f[:, :LANES], chunk)
        lgb_c = _lanes(lg_b[:, :LANES], chunk)
        dmat_ref[...] = jnp.where(dist >= 0.0,
                                  jnp.exp(lgf_c * jnp.maximum(dist, 0.0)),
                                  jnp.exp(lgb_c * jnp.maximum(-dist, 0.0)))

    if zero_init:
        sf_ref[...] = jnp.zeros_like(sf_ref)
        sb_ref[...] = jnp.zeros_like(sb_ref)
    else:
        sf_ref[...] = sf0_ref[...]
        sb_ref[...] = sb0_ref[...]

    def rows(n):
        return slice(n * chunk, (n + 1) * chunk)

    for n in range(n_chunks):
        kn = k_ref[rows(n), :].astype(F32)
        vn = v_ref[rows(n), :]
        incf_ref[n] = lax.dot_general((kn * dec_ref[_KDEC_F]).astype(BF16), vn, contract_rows,
                                      preferred_element_type=F32)
        incb_ref[n] = lax.dot_general((kn * dec_ref[_KDEC_B]).astype(BF16), vn, contract_rows,
                                      preferred_element_type=F32)

    for n in range(n_chunks):
        if emit_out:
            sfs_ref[n] = sf_ref[...].astype(BF16)
        sf_ref[...] = gc_f * sf_ref[...] + incf_ref[n]
    for n in reversed(range(n_chunks)):
        if emit_out:
            sbs_ref[n] = sb_ref[...].astype(BF16)
        sb_ref[...] = gc_b * sb_ref[...] + incb_ref[n]

    if emit_out:
        for n in range(n_chunks):
            r = rows(n)
            qn, kn, vn = q_ref[r, :], k_ref[r, :], v_ref[r, :]
            scores = lax.dot_general(qn, kn, contract_last, preferred_element_type=F32) * dmat_ref[...]
            o = jnp.dot(scores.astype(BF16), vn, preferred_element_type=F32)
            o = o + jnp.dot(qn, sfs_ref[n], preferred_element_type=F32) * dec_ref[_QDEC_F]
            o = o + jnp.dot(qn, sbs_ref[n], preferred_element_type=F32) * dec_ref[_QDEC_B]
            y_ref[r, :] = (_layer_norm_rows(o) * g_ref[r, :].astype(F32)).astype(BF16)

    if emit_state:
        sfo_ref[...] = sf_ref[...]
        sbo_ref[...] = sb_ref[...]


def _retention(p, decay, batch, seq_len, states, emit_out, emit_state):
    m = batch * seq_len
    chunk = min(RET_CHUNK, seq_len)

    def group_spec(g):
        return pl.BlockSpec((None, seq_len, RET_DK), lambda b, h: (g, b, h))

    state_spec = pl.BlockSpec((None, None, RET_DK, RET_DV), lambda b, h: (b, h, 0, 0))
    in_specs = [group_spec(G_RK), group_spec(G_RV)]
    args = [p, p]
    if emit_out:
        in_specs += [group_spec(G_RQ), group_spec(G_RG)]
        args += [p, p]
    in_specs += [pl.BlockSpec((None, 1, LANES), lambda b, h: (h, 0, 0)),
                 pl.BlockSpec((None, 1, LANES), lambda b, h: (RET_HEADS + h, 0, 0))]
    args += [decay, decay]
    if states is not None:
        in_specs += [state_spec, state_spec]
        args += list(states)
    out_specs, out_shape = [], []
    if emit_out:
        out_specs.append(pl.BlockSpec((seq_len, RET_DV), lambda b, h: (b, h)))
        out_shape.append(jax.ShapeDtypeStruct((m, RET_HEADS * RET_DV), BF16))
    if emit_state:
        out_specs += [state_spec, state_spec]
        out_shape += [jax.ShapeDtypeStruct((batch, RET_HEADS, RET_DK, RET_DV), F32)] * 2
    n_chunks = seq_len // chunk
    state = (RET_DK, RET_DV)
    scratch = [pltpu.VMEM(state, F32), pltpu.VMEM(state, F32),
               pltpu.VMEM((4, chunk, RET_DK), F32),
               pltpu.VMEM((n_chunks,) + state, F32),
               pltpu.VMEM((n_chunks,) + state, F32)]
    if emit_out:
        scratch += [pltpu.VMEM((chunk, chunk), F32),
                    pltpu.VMEM((n_chunks,) + state, BF16),
                    pltpu.VMEM((n_chunks,) + state, BF16)]
    outs = pl.pallas_call(
        functools.partial(_ret_kernel, seq_len=seq_len, chunk=chunk, zero_init=states is None,
                          emit_out=emit_out, emit_state=emit_state),
        grid=(batch, RET_HEADS),
        in_specs=in_specs,
        out_specs=out_specs,
        out_shape=out_shape,
        scratch_shapes=scratch,
        compiler_params=_params("arbitrary", "arbitrary"),
        name="retention_%d" % seq_len,
    )(*args)
    outs = list(outs)
    y = outs.pop(0) if emit_out else None
    st = tuple(outs) if emit_state else None
    return y, st


def _att_kernel(*refs, n_lat, tk, lam_init):
    refs = list(refs)
    q_ref, kc_ref, vc_ref = refs.pop(0), refs.pop(0), refs.pop(0)
    kl_ref = vl_ref = None
    if n_lat:
        kl_ref, vl_ref = refs.pop(0), refs.pop(0)
    g_ref, lam_ref, o_ref = refs.pop(0), refs.pop(0), refs.pop(0)
    vt_ref, pt_ref, mc_ref, lc_ref, acc_ref = refs

    tq = q_ref.shape[0]
    t_ctx = kc_ref.shape[0]
    n_chunks = 1 + n_lat
    sub = 8
    contract_last = (((1,), (1,)), ((), ()))

    def span(c):
        return (0, t_ctx) if c == 0 else (t_ctx + (c - 1) * tk, t_ctx + c * tk)

    def keys(c):
        return kc_ref[...] if c == 0 else kl_ref[(c - 1) * tk:c * tk, :]

    @pl.when(pl.program_id(2) == 0)
    def _():
        vt_ref[:, 0:t_ctx] = vc_ref[...].astype(F32).T.astype(BF16)
        for c in range(1, n_chunks):
            lo, hi = span(c)
            vt_ref[:, lo:hi] = vl_ref[(c - 1) * tk:c * tk, :].astype(F32).T.astype(BF16)

    q = q_ref[...]
    lane = lax.broadcasted_iota(jnp.int32, q.shape, 1)
    zero = jnp.zeros_like(q)
    qm = (jnp.where(lane < ATT_DH, q, zero), jnp.where(lane >= ATT_DH, q, zero))

    for c in range(n_chunks):
        lo, hi = span(c)
        k = keys(c)
        for j in range(2):
            st = lax.dot_general(k, qm[j], contract_last, preferred_element_type=F32)
            st = st.reshape((hi - lo) // sub, sub, tq)
            m8 = jnp.max(st, axis=0)
            p = jnp.exp2(st - m8[None])
            mc_ref[j, c] = m8
            lc_ref[j, c] = jnp.sum(p, axis=0)
            pt_ref[j, lo:hi, :] = p.reshape(hi - lo, tq).astype(BF16)

    denom = []
    for j in range(2):
        m = functools.reduce(jnp.maximum, [mc_ref[j, c] for c in range(n_chunks)])
        m = jnp.max(m, axis=0, keepdims=True)
        part = None
        for c in range(n_chunks):
            w = jnp.exp2(mc_ref[j, c] - m)
            mc_ref[j, c] = w
            part = w * lc_ref[j, c] if part is None else part + w * lc_ref[j, c]
        denom.append(jnp.sum(part, axis=0, keepdims=True))
    l0, l1 = denom

    lv = lam_ref[...]
    lam = (jnp.exp(jnp.sum(lv[0:1, :] * lv[1:2, :], axis=-1, keepdims=True))
           - jnp.exp(jnp.sum(lv[2:3, :] * lv[3:4, :], axis=-1, keepdims=True)) + lam_init)
    c1 = lam * l0 / l1

    pack = BF16_SUBLANES
    for c in range(n_chunks):
        lo, hi = span(c)
        w0 = mc_ref[0, c]
        w1 = -c1 * mc_ref[1, c]
        coef0 = jnp.concatenate([w0, w0], axis=0).astype(BF16)
        coef1 = jnp.concatenate([w1, w1], axis=0).astype(BF16)
        p0 = pt_ref[0, lo:hi, :].reshape((hi - lo) // pack, pack, tq)
        p1 = pt_ref[1, lo:hi, :].reshape((hi - lo) // pack, pack, tq)
        a = (p0 * coef0[None] + p1 * coef1[None]).reshape(hi - lo, tq)
        pv = jnp.dot(vt_ref[:, lo:hi], a, preferred_element_type=F32)
        if c == 0:
            acc_ref[...] = pv
        else:
            acc_ref[...] += pv

    ot = acc_ref[...] / l0
    ms = jnp.mean(ot * ot, axis=0, keepdims=True)
    ot = ot * lax.rsqrt(ms + LN_EPS) * (1.0 - lam_init)
    o_ref[...] = (ot.T * g_ref[...].astype(F32)).astype(BF16)


def _attention(p_q, p_ctx, p_lat, lam_rows, batch, q_len, ctx_len, lat_len, lam_init):
    m = batch * q_len
    tq = min(ATT_TQ, q_len)
    nq = q_len // tq
    tk = ATT_TK
    n_lat = 0 if p_lat is None else lat_len // tk
    n_keys = ctx_len + n_lat * tk
    width = 2 * ATT_DH

    def q_spec(g):
        return pl.BlockSpec((None, tq, width), lambda b, h, i: (g, b * nq + i, h))

    def kv_spec(g, length):
        return pl.BlockSpec((None, length, width), lambda b, h, i: (g, b, h))

    in_specs = [q_spec(G_AQ), kv_spec(G_AK, ctx_len), kv_spec(G_AV, ctx_len)]
    args = [p_q, p_ctx, p_ctx]
    if n_lat:
        in_specs += [kv_spec(G_AK, lat_len), kv_spec(G_AV, lat_len)]
        args += [p_lat, p_lat]
    in_specs += [q_spec(G_AG), pl.BlockSpec(lam_rows.shape, lambda b, h, i: (0, 0))]
    args += [p_q, lam_rows]
    n_chunks = 1 + n_lat
    scratch = [pltpu.VMEM((width, n_keys), BF16),
               pltpu.VMEM((2, n_keys, tq), BF16),
               pltpu.VMEM((2, n_chunks, 8, tq), F32),
               pltpu.VMEM((2, n_chunks, 8, tq), F32),
               pltpu.VMEM((width, tq), F32)]
    return pl.pallas_call(
        functools.partial(_att_kernel, n_lat=n_lat, tk=tk, lam_init=lam_init),
        grid=(batch, ATT_HEADS, nq),
        in_specs=in_specs,
        out_specs=pl.BlockSpec((tq, width), lambda b, h, i: (b * nq + i, h)),
        out_shape=jax.ShapeDtypeStruct((m, ATT_HEADS * width), BF16),
        scratch_shapes=scratch,
        compiler_params=_params("arbitrary", "arbitrary", "arbitrary"),
        name="diff_attention_%d" % q_len,
    )(*args)


def _merge_kernel(yr_ref, ya_ref, cb_ref, cc_ref, cx_ref, cg_ref, ccp_ref, cxp_ref, ccn_ref, cxn_ref,
                  gr_ref, gc_ref, ga_ref, x_ref, gate_ref, cw_ref, lng_ref, lnb_ref,
                  wr_ref, wc_ref, wa_ref, wo_ref, o_ref, *, tiles_per_seq):
    i = pl.program_id(0)
    tm = x_ref.shape[0]
    t = i % tiles_per_seq
    u = cc_ref[...].astype(F32) * cx_ref[...].astype(F32)
    halo = BF16_SUBLANES
    u_before = ccp_ref[halo - 1:halo, :].astype(F32) * cxp_ref[halo - 1:halo, :].astype(F32)
    u_after = ccn_ref[0:1, :].astype(F32) * cxn_ref[0:1, :].astype(F32)
    u_before = jnp.where(t == 0, 0.0, u_before)
    u_after = jnp.where(t == tiles_per_seq - 1, 0.0, u_after)
    row = lax.broadcasted_iota(jnp.int32, u.shape, 0)
    u_prev = jnp.where(row == 0, u_before, pltpu.roll(u, 1, axis=0))
    u_next = jnp.where(row == tm - 1, u_after, pltpu.roll(u, tm - 1, axis=0))
    cw = cw_ref[...]
    conv = cw[0:1, :] * u_prev + cw[1:2, :] * u + cw[2:3, :] * u_next
    y_conv = (cb_ref[...].astype(F32) * conv * cg_ref[...].astype(F32)).astype(BF16)

    mix = gr_ref[...].astype(F32) * jnp.dot(yr_ref[...], wr_ref[...], preferred_element_type=F32)
    mix = mix + gc_ref[...].astype(F32) * jnp.dot(y_conv, wc_ref[...], preferred_element_type=F32)
    mix = mix + ga_ref[...].astype(F32) * jnp.dot(ya_ref[...], wa_ref[...], preferred_element_type=F32)
    out = jnp.dot(mix.astype(BF16), wo_ref[...], preferred_element_type=F32)
    z = DEEPNORM_ALPHA * x_ref[...] + gate_ref[...] * out
    o_ref[...] = _layer_norm_rows(z) * lng_ref[...] + lnb_ref[...]


def _merge(p, y_ret, y_att, x2d, mod_rows, mod_row_of_tile, conv_w, ln_g, ln_b, w_r, w_c, w_a, w_o, seq_len):
    m, d = x2d.shape
    tm = min(MERGE_TM, seq_len)
    tiles_per_seq = seq_len // tm
    halo = BF16_SUBLANES
    hb = tm // halo
    n_halo = m // halo

    def group_spec(g):
        return pl.BlockSpec((None, tm, d), lambda i: (g, i, 0))

    def before_spec(g):
        return pl.BlockSpec((None, halo, d), lambda i: (g, jnp.maximum(i * hb - 1, 0), 0))

    def after_spec(g):
        return pl.BlockSpec((None, halo, d), lambda i: (g, jnp.minimum((i + 1) * hb, n_halo - 1), 0))

    row_spec = pl.BlockSpec((tm, d), lambda i: (i, 0))
    vec_spec = pl.BlockSpec((1, d), lambda i: (0, 0))
    w_spec = pl.BlockSpec((d, d), lambda i: (0, 0))
    in_specs = [row_spec, row_spec,
                group_spec(G_CB), group_spec(G_CC), group_spec(G_CX), group_spec(G_CG),
                before_spec(G_CC), before_spec(G_CX), after_spec(G_CC), after_spec(G_CX),
                group_spec(G_MR), group_spec(G_MC), group_spec(G_MA),
                row_spec,
                pl.BlockSpec((None, 1, d), lambda i: (mod_row_of_tile(i), 0, 2)),
                pl.BlockSpec((3, d), lambda i: (0, 0)), vec_spec, vec_spec,
                w_spec, w_spec, w_spec, w_spec]
    args = [y_ret, y_att, p, p, p, p, p, p, p, p, p, p, p, x2d, mod_rows, conv_w,
            ln_g.reshape(1, d), ln_b.reshape(1, d), w_r, w_c, w_a, w_o]
    return pl.pallas_call(
        functools.partial(_merge_kernel, tiles_per_seq=tiles_per_seq),
        grid=(m // tm,),
        in_specs=in_specs,
        out_specs=row_spec,
        out_shape=jax.ShapeDtypeStruct((m, d), F32),
        compiler_params=_params("arbitrary"),
        name="merge_%d" % seq_len,
    )(*args)


def _rope_tables(seq_len):
    pos = jnp.arange(seq_len, dtype=jnp.int32)
    rows = (pos // GRID_W).astype(F32)
    cols = (pos % GRID_W).astype(F32)

    def angles(head_dim):
        n_freq = head_dim // 4
        inv = ROPE_BASE ** (-jnp.arange(n_freq, dtype=F32) / n_freq)
        return jnp.concatenate([rows[:, None] * inv, cols[:, None] * inv], axis=-1)

    ang_r = angles(RET_DK)
    ang_a = angles(ATT_DH)
    cos_a, sin_a = jnp.cos(ang_a), jnp.sin(ang_a)
    cos_att = jnp.concatenate([cos_a] * 4, axis=-1)
    sin_att = jnp.concatenate([-sin_a, sin_a, -sin_a, sin_a], axis=-1)
    return jnp.cos(ang_r), jnp.sin(ang_r), cos_att, sin_att


def kernel(x, c, ctx, c_ctx, w_mod, b_mod, w_in, ret_decay, conv_w, diff_lambda,
           w_ret_out, w_conv_out, w_att_out, w_out, ln_g, ln_b):
    bsz, t_lat, d = x.shape
    t_ctx = ctx.shape[1]
    depth = w_mod.shape[0]
    ctx_row = bsz

    c_rows = jnp.concatenate([c, c_ctx[None, :], jnp.zeros((8 - bsz - 1, d), F32)], axis=0)
    mod = _modulation(c_rows, w_mod, b_mod)
    rope_tabs = _rope_tables(t_lat)
    w_in_bf = w_in.astype(BF16)
    w_r, w_c, w_a, w_o = (w.astype(BF16) for w in (w_ret_out, w_conv_out, w_att_out, w_out))
    decay = jnp.broadcast_to(ret_decay.astype(F32).reshape(depth, 2 * RET_HEADS, 1, 1),
                             (depth, 2 * RET_HEADS, 1, LANES))

    lat_tiles_proj = t_lat // min(PROJ_TM, t_lat)
    lat_tiles_merge = t_lat // min(MERGE_TM, t_lat)
    x2d = x.reshape(bsz * t_lat, d)
    xc2d = ctx.reshape(bsz * t_ctx, d)
    for l in range(depth):
        last = l == depth - 1
        lam_init = 0.8 - 0.6 * math.exp(-0.3 * l)
        mod_rows = mod[l].reshape(8, 1, 3 * d)
        lam_rows = diff_lambda[l].astype(F32)

        n_ctx_groups = N_KV_GROUPS if last else N_GROUPS
        pc = _projection(xc2d, mod_rows, lambda i: ctx_row, w_in_bf, l, n_ctx_groups, t_ctx, None)
        cy_ret, states = _retention(pc, decay[l], bsz, t_ctx, None, emit_out=not last, emit_state=True)

        p = _projection(x2d, mod_rows, lambda i: i // lat_tiles_proj, w_in_bf, l, N_GROUPS, t_lat, rope_tabs)
        y_ret, _ = _retention(p, decay[l], bsz, t_lat, states, emit_out=True, emit_state=False)
        y_att = _attention(p, pc, p, lam_rows, bsz, t_lat, t_ctx, t_lat, lam_init)
        x_new = _merge(p, y_ret, y_att, x2d, mod_rows, lambda i: i // lat_tiles_merge, conv_w[l],
                       ln_g[l], ln_b[l], w_r[l], w_c[l], w_a[l], w_o[l], t_lat)
        if not last:
            cy_att = _attention(pc, pc, None, lam_rows, bsz, t_ctx, t_ctx, 0, lam_init)
            xc2d = _merge(pc, cy_ret, cy_att, xc2d, mod_rows, lambda i: ctx_row, conv_w[l],
                          ln_g[l], ln_b[l], w_r[l], w_c[l], w_a[l], w_o[l], t_ctx)
        x2d = x_new
    return x2d.reshape(bsz, t_lat, d)
```

```python
import functools
import math

import jax
import jax.numpy as jnp
from jax import lax
from jax.experimental import pallas as pl
from jax.experimental.pallas import tpu as pltpu

F32 = jnp.float32
BF16 = jnp.bfloat16

D_MODEL = 1024
DEPTH = 4
GRID_W = 64
RET_HEADS = 4
RET_DK = 256
RET_DV = 256
ATT_HEADS = 8
ATT_DH = 64
ROPE_BASE = 10000.0
LN_EPS = 1e-6
DEEPNORM_ALPHA = (2 * DEPTH) ** 0.25
LOG2_E = math.log2(math.e)

G_RK, G_RV, G_AK, G_AV, G_RQ, G_RG, G_AQ, G_AG, G_CB, G_CC, G_CX, G_CG, G_MR, G_MC, G_MA = range(15)
N_GROUPS = 15
N_KV_GROUPS = 4

LANES = 128
BF16_SUBLANES = 16
VMEM_LIMIT = 56 * 1024 * 1024

RET_CHUNK = 256
PROJ_TM = 2048
ATT_TQ = 1024
ATT_TK = 256
MERGE_TM = 512


def _sigmoid(x):
    return 0.5 * jnp.tanh(0.5 * x) + 0.5


def _layer_norm_rows(x):
    mu = jnp.mean(x, axis=-1, keepdims=True)
    xc = x - mu
    var = jnp.mean(xc * xc, axis=-1, keepdims=True)
    return xc * lax.rsqrt(var + LN_EPS)


def _params(*sem):
    return pltpu.CompilerParams(dimension_semantics=sem, vmem_limit_bytes=VMEM_LIMIT)


def _mod_kernel(c_ref, w_ref, b_ref, o_ref):
    c = c_ref[...]
    s = (c * _sigmoid(c)).astype(BF16)
    o_ref[...] = jnp.dot(s, w_ref[...].astype(BF16), preferred_element_type=F32) + b_ref[...]


def _modulation(c_rows, w_mod, b_mod):
    depth, d, d3 = w_mod.shape
    tn = 1024
    return pl.pallas_call(
        _mod_kernel,
        grid=(depth, d3 // tn),
        in_specs=[
            pl.BlockSpec((8, d), lambda l, j: (0, 0)),
            pl.BlockSpec((None, d, tn), lambda l, j: (l, 0, j)),
            pl.BlockSpec((None, 1, tn), lambda l, j: (l, 0, j)),
        ],
        out_specs=pl.BlockSpec((None, 8, tn), lambda l, j: (l, 0, j)),
        out_shape=jax.ShapeDtypeStruct((depth, 8, d3), F32),
        compiler_params=_params("arbitrary", "arbitrary"),
        name="modulation",
    )(c_rows, w_mod, b_mod.reshape(depth, 1, d3))


def _proj_kernel(*refs, rope):
    if rope:
        x_ref, sh_ref, sc_ref, w_ref, cr_ref, sr_ref, ca_ref, sa_ref, o_ref, h_ref = refs
    else:
        x_ref, sh_ref, sc_ref, w_ref, o_ref, h_ref = refs
    j = pl.program_id(1)

    @pl.when(j == 0)
    def _():
        y = _layer_norm_rows(x_ref[...])
        h_ref[...] = (y * (1.0 + sc_ref[...]) + sh_ref[...]).astype(BF16)

    def product():
        return jnp.dot(h_ref[...], w_ref[...], preferred_element_type=F32)

    def store_rope_ret(scale):
        acc = product()
        c = cr_ref[...]
        s = sr_ref[...]
        for hh in range(RET_HEADS):
            lo = hh * RET_DK
            mid = lo + RET_DK // 2
            hi = lo + RET_DK
            x1 = acc[:, lo:mid]
            x2 = acc[:, mid:hi]
            o_ref[:, lo:mid] = ((x1 * c - x2 * s) * scale).astype(BF16)
            o_ref[:, mid:hi] = ((x1 * s + x2 * c) * scale).astype(BF16)

    def store_rope_att(scale):
        acc = product()
        c = ca_ref[...]
        s = sa_ref[...]
        lane = lax.broadcasted_iota(jnp.int32, c.shape, 1)
        first_half = (lane % ATT_DH) < (ATT_DH // 2)
        for hh in range(ATT_HEADS):
            lo = hh * LANES
            x = acc[:, lo:lo + LANES]
            partner = jnp.where(first_half,
                                pltpu.roll(x, LANES - ATT_DH // 2, axis=1),
                                pltpu.roll(x, ATT_DH // 2, axis=1))
            o_ref[:, lo:lo + LANES] = ((x * c + partner * s) * scale).astype(BF16)

    def store_scaled(scale):
        o_ref[...] = (product() * scale).astype(BF16)

    def store_plain():
        o_ref[...] = product().astype(BF16)

    def store_silu():
        a = product()
        o_ref[...] = (a * _sigmoid(a)).astype(BF16)

    def store_sigmoid():
        o_ref[...] = _sigmoid(product()).astype(BF16)

    q_r_scale = RET_DK ** -0.5
    q_a_scale = ATT_DH ** -0.5 * LOG2_E
    if rope:
        pl.when(j == G_RK)(functools.partial(store_rope_ret, 1.0))
        pl.when(j == G_RQ)(functools.partial(store_rope_ret, q_r_scale))
        pl.when(j == G_AK)(functools.partial(store_rope_att, 1.0))
        pl.when(j == G_AQ)(functools.partial(store_rope_att, q_a_scale))
        plain = (G_RV, G_AV, G_CB, G_CC, G_CX)
    else:
        pl.when(j == G_RQ)(functools.partial(store_scaled, q_r_scale))
        pl.when(j == G_AQ)(functools.partial(store_scaled, q_a_scale))
        plain = (G_RK, G_AK, G_RV, G_AV, G_CB, G_CC, G_CX)
    is_plain = functools.reduce(jnp.logical_or, [j == g for g in plain])
    pl.when(is_plain)(store_plain)
    is_silu = functools.reduce(jnp.logical_or, [j == g for g in (G_RG, G_AG, G_CG)])
    pl.when(is_silu)(store_silu)
    pl.when(j >= G_MR)(store_sigmoid)


def _projection(x2d, mod_rows, mod_row_of_tile, w_bf16, layer, n_groups, seq_len, rope_tabs):
    m, d = x2d.shape
    rope = rope_tabs is not None
    tm = min(PROJ_TM, seq_len if rope else m)
    tiles_per_seq = max(seq_len // tm, 1)
    in_specs = [
        pl.BlockSpec((tm, d), lambda i, j: (i, 0)),
        pl.BlockSpec((None, 1, d), lambda i, j: (mod_row_of_tile(i), 0, 0)),
        pl.BlockSpec((None, 1, d), lambda i, j: (mod_row_of_tile(i), 0, 1)),
        pl.BlockSpec((None, d, d), lambda i, j: (layer, 0, j)),
    ]
    args = [x2d, mod_rows, mod_rows, w_bf16]
    if rope:
        tab_spec = pl.BlockSpec((tm, LANES), lambda i, j: (i % tiles_per_seq, 0))
        in_specs += [tab_spec] * 4
        args += list(rope_tabs)
    return pl.pallas_call(
        functools.partial(_proj_kernel, rope=rope),
        grid=(m // tm, n_groups),
        in_specs=in_specs,
        out_specs=pl.BlockSpec((None, tm, d), lambda i, j: (j, i, 0)),
        out_shape=jax.ShapeDtypeStruct((n_groups, m, d), BF16),
        scratch_shapes=[pltpu.VMEM((tm, d), BF16)],
        compiler_params=_params("arbitrary", "arbitrary"),
        name="projection_rope" if rope else "projection_ctx",
    )(*args)


def _lanes(x, n):
    return x if n == LANES else jnp.concatenate([x] * (n // LANES), axis=1)


_KDEC_F, _KDEC_B, _QDEC_F, _QDEC_B = range(4)


def _ret_kernel(*refs, seq_len, chunk, zero_init, emit_out, emit_state):
    refs = list(refs)
    k_ref, v_ref = refs.pop(0), refs.pop(0)
    q_ref = g_ref = None
    if emit_out:
        q_ref, g_ref = refs.pop(0), refs.pop(0)
    df_ref, db_ref = refs.pop(0), refs.pop(0)
    sf0_ref = sb0_ref = None
    if not zero_init:
        sf0_ref, sb0_ref = refs.pop(0), refs.pop(0)
    y_ref = sfo_ref = sbo_ref = None
    if emit_out:
        y_ref = refs.pop(0)
    if emit_state:
        sfo_ref, sbo_ref = refs.pop(0), refs.pop(0)
    sf_ref, sb_ref, dec_ref, incf_ref, incb_ref = (refs.pop(0) for _ in range(5))
    dmat_ref = sfs_ref = sbs_ref = None
    if emit_out:
        dmat_ref, sfs_ref, sbs_ref = refs.pop(0), refs.pop(0), refs.pop(0)

    n_chunks = seq_len // chunk
    contract_rows = (((0,), (0,)), ((), ()))
    contract_last = (((1,), (1,)), ((), ()))

    lg_f = _lanes(-jnp.exp(df_ref[...]), RET_DK)
    lg_b = _lanes(-jnp.exp(db_ref[...]), RET_DK)
    pos = lax.broadcasted_iota(jnp.int32, (chunk, RET_DK), 0).astype(F32)
    dec_ref[_KDEC_F] = jnp.exp(lg_f * (chunk - 1.0 - pos))
    dec_ref[_KDEC_B] = jnp.exp(lg_b * pos)
    gc_f = jnp.exp(lg_f * float(chunk))
    gc_b = jnp.exp(lg_b * float(chunk))
    if emit_out:
        dec_ref[_QDEC_F] = jnp.exp(lg_f * (pos + 1.0))
        dec_ref[_QDEC_B] = jnp.exp(lg_b * (float(chunk) - pos))
        row = lax.broadcasted_iota(jnp.int32, (chunk, chunk), 0)
        col = lax.broadcasted_iota(jnp.int32, (chunk, chunk), 1)
        dist = (row - col).astype(F32)
        lgf_c = _lanes(lg_f[:, :LANES], chunk)
        lgb_c = _lanes(lg_b[:, :LANES], chunk)
        dmat_ref[...] = jnp.where(dist >= 0.0,
                                  jnp.exp(lgf_c * jnp.maximum(dist, 0.0)),
                                  jnp.exp(lgb_c * jnp.maximum(-dist, 0.0)))

    if zero_init:
        sf_ref[...] = jnp.zeros_like(sf_ref)
        sb_ref[...] = jnp.zeros_like(sb_ref)
    else:
        sf_ref[...] = sf0_ref[...]
        sb_ref[...] = sb0_ref[...]

    def rows(n):
        return slice(n * chunk, (n + 1) * chunk)

    for n in range(n_chunks):
        kn = k_ref[rows(n), :].astype(F32)
        vn = v_ref[rows(n), :]
        incf_ref[n] = lax.dot_general((kn * dec_ref[_KDEC_F]).astype(BF16), vn, contract_rows,
                                      preferred_element_type=F32)
        incb_ref[n] = lax.dot_general((kn * dec_ref[_KDEC_B]).astype(BF16), vn, contract_rows,
                                      preferred_element_type=F32)

    for n in range(n_chunks):
        if emit_out:
            sfs_ref[n] = sf_ref[...].astype(BF16)
        sf_ref[...] = gc_f * sf_ref[...] + incf_ref[n]
    for n in reversed(range(n_chunks)):
        if emit_out:
            sbs_ref[n] = sb_ref[...].astype(BF16)
        sb_ref[...] = gc_b * sb_ref[...] + incb_ref[n]

    if emit_out:
        for n in range(n_chunks):
            r = rows(n)
            qn, kn, vn = q_ref[r, :], k_ref[r, :], v_ref[r, :]
            scores = lax.dot_general(qn, kn, contract_last, preferred_element_type=F32) * dmat_ref[...]
            o = jnp.dot(scores.astype(BF16), vn, preferred_element_type=F32)
            o = o + jnp.dot(qn, sfs_ref[n], preferred_element_type=F32) * dec_ref[_QDEC_F]
            o = o + jnp.dot(qn, sbs_ref[n], preferred_element_type=F32) * dec_ref[_QDEC_B]
            y_ref[r, :] = (_layer_norm_rows(o) * g_ref[r, :].astype(F32)).astype(BF16)

    if emit_state:
        sfo_ref[...] = sf_ref[...]
        sbo_ref[...] = sb_ref[...]


def _retention(p, decay, batch, seq_len, states, emit_out, emit_state):
    m = batch * seq_len
    chunk = min(RET_CHUNK, seq_len)

    def group_spec(g):
        return pl.BlockSpec((None, seq_len, RET_DK), lambda b, h: (g, b, h))

    state_spec = pl.BlockSpec((None, None, RET_DK, RET_DV), lambda b, h: (b, h, 0, 0))
    in_specs = [group_spec(G_RK), group_spec(G_RV)]
    args = [p, p]
    if emit_out:
        in_specs += [group_spec(G_RQ), group_spec(G_RG)]
        args += [p, p]
    in_specs += [pl.BlockSpec((None, 1, LANES), lambda b, h: (h, 0, 0)),
                 pl.BlockSpec((None, 1, LANES), lambda b, h: (RET_HEADS + h, 0, 0))]
    args += [decay, decay]
    if states is not None:
        in_specs += [state_spec, state_spec]
        args += list(states)
    out_specs, out_shape = [], []
    if emit_out:
        out_specs.append(pl.BlockSpec((seq_len, RET_DV), lambda b, h: (b, h)))
        out_shape.append(jax.ShapeDtypeStruct((m, RET_HEADS * RET_DV), BF16))
    if emit_state:
        out_specs += [state_spec, state_spec]
        out_shape += [jax.ShapeDtypeStruct((batch, RET_HEADS, RET_DK, RET_DV), F32)] * 2
    n_chunks = seq_len // chunk
    state = (RET_DK, RET_DV)
    scratch = [pltpu.VMEM(state, F32), pltpu.VMEM(state, F32),
               pltpu.VMEM((4, chunk, RET_DK), F32),
               pltpu.VMEM((n_chunks,) + state, F32),
               pltpu.VMEM((n_chunks,) + state, F32)]
    if emit_out:
        scratch += [pltpu.VMEM((chunk, chunk), F32),
                    pltpu.VMEM((n_chunks,) + state, BF16),
                    pltpu.VMEM((n_chunks,) + state, BF16)]
    outs = pl.pallas_call(
        functools.partial(_ret_kernel, seq_len=seq_len, chunk=chunk, zero_init=states is None,
                          emit_out=emit_out, emit_state=emit_state),
        grid=(batch, RET_HEADS),
        in_specs=in_specs,
        out_specs=out_specs,
        out_shape=out_shape,
        scratch_shapes=scratch,
        compiler_params=_params("arbitrary", "arbitrary"),
        name="retention_%d" % seq_len,
    )(*args)
    outs = list(outs)
    y = outs.pop(0) if emit_out else None
    st = tuple(outs) if emit_state else None
    return y, st


def _att_kernel(*refs, n_lat, tk, lam_init):
    refs = list(refs)
    q_ref, kc_ref, vc_ref = refs.pop(0), refs.pop(0), refs.pop(0)
    kl_ref = vl_ref = None
    if n_lat:
        kl_ref, vl_ref = refs.pop(0), refs.pop(0)
    g_ref, lam_ref, o_ref = refs.pop(0), refs.pop(0), refs.pop(0)
    vt_ref, pt_ref, mc_ref, lc_ref, acc_ref = refs

    tq = q_ref.shape[0]
    t_ctx = kc_ref.shape[0]
    n_chunks = 1 + n_lat
    sub = 8
    contract_last = (((1,), (1,)), ((), ()))

    def span(c):
        return (0, t_ctx) if c == 0 else (t_ctx + (c - 1) * tk, t_ctx + c * tk)

    def keys(c):
        return kc_ref[...] if c == 0 else kl_ref[(c - 1) * tk:c * tk, :]

    @pl.when(pl.program_id(2) == 0)
    def _():
        vt_ref[:, 0:t_ctx] = vc_ref[...].astype(F32).T.astype(BF16)
        for c in range(1, n_chunks):
            lo, hi = span(c)
            vt_ref[:, lo:hi] = vl_ref[(c - 1) * tk:c * tk, :].astype(F32).T.astype(BF16)

    q = q_ref[...]
    lane = lax.broadcasted_iota(jnp.int32, q.shape, 1)
    zero = jnp.zeros_like(q)
    qm = (jnp.where(lane < ATT_DH, q, zero), jnp.where(lane >= ATT_DH, q, zero))

    for c in range(n_chunks):
        lo, hi = span(c)
        k = keys(c)
        for j in range(2):
            st = lax.dot_general(k, qm[j], contract_last, preferred_element_type=F32)
            st = st.reshape((hi - lo) // sub, sub, tq)
            m8 = jnp.max(st, axis=0)
            p = jnp.exp2(st - m8[None])
            mc_ref[j, c] = m8
            lc_ref[j, c] = jnp.sum(p, axis=0)
            pt_ref[j, lo:hi, :] = p.reshape(hi - lo, tq).astype(BF16)

    denom = []
    for j in range(2):
        m = functools.reduce(jnp.maximum, [mc_ref[j, c] for c in range(n_chunks)])
        m = jnp.max(m, axis=0, keepdims=True)
        part = None
        for c in range(n_chunks):
            w = jnp.exp2(mc_ref[j, c] - m)
            mc_ref[j, c] = w
            part = w * lc_ref[j, c] if part is None else part + w * lc_ref[j, c]
        denom.append(jnp.sum(part, axis=0, keepdims=True))
    l0, l1 = denom

    lv = lam_ref[...]
    lam = (jnp.exp(jnp.sum(lv[0:1, :] * lv[1:2, :], axis=-1, keepdims=True))
           - jnp.exp(jnp.sum(lv[2:3, :] * lv[3:4, :], axis=-1, keepdims=True)) + lam_init)
    c1 = lam * l0 / l1

    pack = BF16_SUBLANES
    for c in range(n_chunks):
        lo, hi = span(c)
        w0 = mc_ref[0, c]
        w1 = -c1 * mc_ref[1, c]
        coef0 = jnp.concatenate([w0, w0], axis=0).astype(BF16)
        coef1 = jnp.concatenate([w1, w1], axis=0).astype(BF16)
        p0 = pt_ref[0, lo:hi, :].reshape((hi - lo) // pack, pack, tq)
        p1 = pt_ref[1, lo:hi, :].reshape((hi - lo) // pack, pack, tq)
        a = (p0 * coef0[None] + p1 * coef1[None]).reshape(hi - lo, tq)
        pv = jnp.dot(vt_ref[:, lo:hi], a, preferred_element_type=F32)
        if c == 0:
            acc_ref[...] = pv
        else:
            acc_ref[...] += pv

    ot = acc_ref[...] / l0
    ms = jnp.mean(ot * ot, axis=0, keepdims=True)
    ot = ot * lax.rsqrt(ms + LN_EPS) * (1.0 - lam_init)
    o_ref[...] = (ot.T * g_ref[...].astype(F32)).astype(BF16)


def _attention(p_q, p_ctx, p_lat, lam_rows, batch, q_len, ctx_len, lat_len, lam_init):
    m = batch * q_len
    tq = min(ATT_TQ, q_len)
    nq = q_len // tq
    tk = ATT_TK
    n_lat = 0 if p_lat is None else lat_len // tk
    n_keys = ctx_len + n_lat * tk
    width = 2 * ATT_DH

    def q_spec(g):
        return pl.BlockSpec((None, tq, width), lambda b, h, i: (g, b * nq + i, h))

    def kv_spec(g, length):
        return pl.BlockSpec((None, length, width), lambda b, h, i: (g, b, h))

    in_specs = [q_spec(G_AQ), kv_spec(G_AK, ctx_len), kv_spec(G_AV, ctx_len)]
    args = [p_q, p_ctx, p_ctx]
    if n_lat:
        in_specs += [kv_spec(G_AK, lat_len), kv_spec(G_AV, lat_len)]
        args += [p_lat, p_lat]
    in_specs += [q_spec(G_AG), pl.BlockSpec(lam_rows.shape, lambda b, h, i: (0, 0))]
    args += [p_q, lam_rows]
    n_chunks = 1 + n_lat
    scratch = [pltpu.VMEM((width, n_keys), BF16),
               pltpu.VMEM((2, n_keys, tq), BF16),
               pltpu.VMEM((2, n_chunks, 8, tq), F32),
               pltpu.VMEM((2, n_chunks, 8, tq), F32),
               pltpu.VMEM((width, tq), F32)]
    return pl.pallas_call(
        functools.partial(_att_kernel, n_lat=n_lat, tk=tk, lam_init=lam_init),
        grid=(batch, ATT_HEADS, nq),
        in_specs=in_specs,
        out_specs=pl.BlockSpec((tq, width), lambda b, h, i: (b * nq + i, h)),
        out_shape=jax.ShapeDtypeStruct((m, ATT_HEADS * width), BF16),
        scratch_shapes=scratch,
        compiler_params=_params("arbitrary", "arbitrary", "arbitrary"),
        name="diff_attention_%d" % q_len,
    )(*args)


def _merge_kernel(yr_ref, ya_ref, cb_ref, cc_ref, cx_ref, cg_ref, ccp_ref, cxp_ref, ccn_ref, cxn_ref,
                  gr_ref, gc_ref, ga_ref, x_ref, gate_ref, cw_ref, lng_ref, lnb_ref,
                  wr_ref, wc_ref, wa_ref, wo_ref, o_ref, *, tiles_per_seq):
    i = pl.program_id(0)
    tm = x_ref.shape[0]
    t = i % tiles_per_seq
    u = cc_ref[...].astype(F32) * cx_ref[...].astype(F32)
    halo = BF16_SUBLANES
    u_before = ccp_ref[halo - 1:halo, :].astype(F32) * cxp_ref[halo - 1:halo, :].astype(F32)
    u_after = ccn_ref[0:1, :].astype(F32) * cxn_ref[0:1, :].astype(F32)
    u_before = jnp.where(t == 0, 0.0, u_before)
    u_after = jnp.where(t == tiles_per_seq - 1, 0.0, u_after)
    row = lax.broadcasted_iota(jnp.int32, u.shape, 0)
    u_prev = jnp.where(row == 0, u_before, pltpu.roll(u, 1, axis=0))
    u_next = jnp.where(row == tm - 1, u_after, pltpu.roll(u, tm - 1, axis=0))
    cw = cw_ref[...]
    conv = cw[0:1, :] * u_prev + cw[1:2, :] * u + cw[2:3, :] * u_next
    y_conv = (cb_ref[...].astype(F32) * conv * cg_ref[...].astype(F32)).astype(BF16)

    n_parts = 2 if tm % (2 * BF16_SUBLANES) == 0 else 1
    for part in range(n_parts):
        r = slice(part * tm // n_parts, (part + 1) * tm // n_parts)
        mix = gr_ref[r, :].astype(F32) * jnp.dot(yr_ref[r, :], wr_ref[...], preferred_element_type=F32)
        mix = mix + gc_ref[r, :].astype(F32) * jnp.dot(y_conv[r, :], wc_ref[...], preferred_element_type=F32)
        mix = mix + ga_ref[r, :].astype(F32) * jnp.dot(ya_ref[r, :], wa_ref[...], preferred_element_type=F32)
        out = jnp.dot(mix.astype(BF16), wo_ref[...], preferred_element_type=F32)
        z = DEEPNORM_ALPHA * x_ref[r, :] + gate_ref[...] * out
        o_ref[r, :] = _layer_norm_rows(z) * lng_ref[...] + lnb_ref[...]


def _merge(p, y_ret, y_att, x2d, mod_rows, mod_row_of_tile, conv_w, ln_g, ln_b, w_r, w_c, w_a, w_o, seq_len):
    m, d = x2d.shape
    tm = min(MERGE_TM, seq_len)
    tiles_per_seq = seq_len // tm
    halo = BF16_SUBLANES
    hb = tm // halo
    n_halo = m // halo

    def group_spec(g):
        return pl.BlockSpec((None, tm, d), lambda i: (g, i, 0))

    def before_spec(g):
        return pl.BlockSpec((None, halo, d), lambda i: (g, jnp.maximum(i * hb - 1, 0), 0))

    def after_spec(g):
        return pl.BlockSpec((None, halo, d), lambda i: (g, jnp.minimum((i + 1) * hb, n_halo - 1), 0))

    row_spec = pl.BlockSpec((tm, d), lambda i: (i, 0))
    vec_spec = pl.BlockSpec((1, d), lambda i: (0, 0))
    w_spec = pl.BlockSpec((d, d), lambda i: (0, 0), pipeline_mode=pl.Buffered(1))
    in_specs = [row_spec, row_spec,
                group_spec(G_CB), group_spec(G_CC), group_spec(G_CX), group_spec(G_CG),
                before_spec(G_CC), before_spec(G_CX), after_spec(G_CC), after_spec(G_CX),
                group_spec(G_MR), group_spec(G_MC), group_spec(G_MA),
                row_spec,
                pl.BlockSpec((None, 1, d), lambda i: (mod_row_of_tile(i), 0, 2)),
                pl.BlockSpec((3, d), lambda i: (0, 0)), vec_spec, vec_spec,
                w_spec, w_spec, w_spec, w_spec]
    args = [y_ret, y_att, p, p, p, p, p, p, p, p, p, p, p, x2d, mod_rows, conv_w,
            ln_g.reshape(1, d), ln_b.reshape(1, d), w_r, w_c, w_a, w_o]
    return pl.pallas_call(
        functools.partial(_merge_kernel, tiles_per_seq=tiles_per_seq),
        grid=(m // tm,),
        in_specs=in_specs,
        out_specs=row_spec,
        out_shape=jax.ShapeDtypeStruct((m, d), F32),
        compiler_params=_params("arbitrary"),
        name="merge_%d" % seq_len,
    )(*args)


def _rope_tables(seq_len):
    pos = jnp.arange(seq_len, dtype=jnp.int32)
    rows = (pos // GRID_W).astype(F32)
    cols = (pos % GRID_W).astype(F32)

    def angles(head_dim):
        n_freq = head_dim // 4
        inv = ROPE_BASE ** (-jnp.arange(n_freq, dtype=F32) / n_freq)
        return jnp.concatenate([rows[:, None] * inv, cols[:, None] * inv], axis=-1)

    ang_r = angles(RET_DK)
    ang_a = angles(ATT_DH)
    cos_a, sin_a = jnp.cos(ang_a), jnp.sin(ang_a)
    cos_att = jnp.concatenate([cos_a] * 4, axis=-1)
    sin_att = jnp.concatenate([-sin_a, sin_a, -sin_a, sin_a], axis=-1)
    return jnp.cos(ang_r), jnp.sin(ang_r), cos_att, sin_att


def kernel(x, c, ctx, c_ctx, w_mod, b_mod, w_in, ret_decay, conv_w, diff_lambda,
           w_ret_out, w_conv_out, w_att_out, w_out, ln_g, ln_b):
    bsz, t_lat, d = x.shape
    t_ctx = ctx.shape[1]
    depth = w_mod.shape[0]
    ctx_row = bsz

    c_rows = jnp.concatenate([c, c_ctx[None, :], jnp.zeros((8 - bsz - 1, d), F32)], axis=0)
    mod = _modulation(c_rows, w_mod, b_mod)
    rope_tabs = _rope_tables(t_lat)
    w_in_bf = w_in.astype(BF16)
    w_r, w_c, w_a, w_o = (w.astype(BF16) for w in (w_ret_out, w_conv_out, w_att_out, w_out))
    decay = jnp.broadcast_to(ret_decay.astype(F32).reshape(depth, 2 * RET_HEADS, 1, 1),
                             (depth, 2 * RET_HEADS, 1, LANES))

    lat_tiles_proj = t_lat // min(PROJ_TM, t_lat)
    lat_tiles_merge = t_lat // min(MERGE_TM, t_lat)
    x2d = x.reshape(bsz * t_lat, d)
    xc2d = ctx.reshape(bsz * t_ctx, d)
    for l in range(depth):
        last = l == depth - 1
        lam_init = 0.8 - 0.6 * math.exp(-0.3 * l)
        mod_rows = mod[l].reshape(8, 1, 3 * d)
        lam_rows = diff_lambda[l].astype(F32)

        n_ctx_groups = N_KV_GROUPS if last else N_GROUPS
        pc = _projection(xc2d, mod_rows, lambda i: ctx_row, w_in_bf, l, n_ctx_groups, t_ctx, None)
        cy_ret, states = _retention(pc, decay[l], bsz, t_ctx, None, emit_out=not last, emit_state=True)

        p = _projection(x2d, mod_rows, lambda i: i // lat_tiles_proj, w_in_bf, l, N_GROUPS, t_lat, rope_tabs)
        y_ret, _ = _retention(p, decay[l], bsz, t_lat, states, emit_out=True, emit_state=False)
        y_att = _attention(p, pc, p, lam_rows, bsz, t_lat, t_ctx, t_lat, lam_init)
        x_new = _merge(p, y_ret, y_att, x2d, mod_rows, lambda i: i // lat_tiles_merge, conv_w[l],
                       ln_g[l], ln_b[l], w_r[l], w_c[l], w_a[l], w_o[l], t_lat)
        if not last:
            cy_att = _attention(pc, pc, None, lam_rows, bsz, t_ctx, t_ctx, 0, lam_init)
            xc2d = _merge(pc, cy_ret, cy_att, xc2d, mod_rows, lambda i: ctx_row, conv_w[l],
                          ln_g[l], ln_b[l], w_r[l], w_c[l], w_a[l], w_o[l], t_ctx)
        x2d = x_new
    return x2d.reshape(bsz, t_lat, d)
```

```python
import functools
import math

import jax
import jax.numpy as jnp
from jax import lax
from jax.experimental import pallas as pl
from jax.experimental.pallas import tpu as pltpu

F32 = jnp.float32
BF16 = jnp.bfloat16

D_MODEL = 1024
DEPTH = 4
GRID_W = 64
RET_HEADS = 4
RET_DK = 256
RET_DV = 256
ATT_HEADS = 8
ATT_DH = 64
ROPE_BASE = 10000.0
LN_EPS = 1e-6
DEEPNORM_ALPHA = (2 * DEPTH) ** 0.25
LOG2_E = math.log2(math.e)

G_RK, G_RV, G_AK, G_AV, G_RQ, G_RG, G_AQ, G_AG, G_CB, G_CC, G_CX, G_CG, G_MR, G_MC, G_MA = range(15)
N_GROUPS = 15
N_KV_GROUPS = 4

LANES = 128
BF16_SUBLANES = 16
VMEM_LIMIT = 56 * 1024 * 1024

RET_CHUNK = 256
PROJ_TM = 2048
ATT_TQ = 1024
ATT_TK = 256
ATT_STAT_KEYS = 128
ATT_STAT_QUERIES = 256
MERGE_TM = 512


def _sigmoid(x):
    return 0.5 * jnp.tanh(0.5 * x) + 0.5


def _layer_norm_rows(x):
    mu = jnp.mean(x, axis=-1, keepdims=True)
    xc = x - mu
    var = jnp.mean(xc * xc, axis=-1, keepdims=True)
    return xc * lax.rsqrt(var + LN_EPS)


def _params(*sem):
    return pltpu.CompilerParams(dimension_semantics=sem, vmem_limit_bytes=VMEM_LIMIT)


def _mod_kernel(c_ref, w_ref, b_ref, o_ref):
    c = c_ref[...]
    s = (c * _sigmoid(c)).astype(BF16)
    o_ref[...] = jnp.dot(s, w_ref[...].astype(BF16), preferred_element_type=F32) + b_ref[...]


def _modulation(c_rows, w_mod, b_mod):
    depth, d, d3 = w_mod.shape
    tn = 1024
    return pl.pallas_call(
        _mod_kernel,
        grid=(depth, d3 // tn),
        in_specs=[
            pl.BlockSpec((8, d), lambda l, j: (0, 0)),
            pl.BlockSpec((None, d, tn), lambda l, j: (l, 0, j)),
            pl.BlockSpec((None, 1, tn), lambda l, j: (l, 0, j)),
        ],
        out_specs=pl.BlockSpec((None, 8, tn), lambda l, j: (l, 0, j)),
        out_shape=jax.ShapeDtypeStruct((depth, 8, d3), F32),
        compiler_params=_params("arbitrary", "arbitrary"),
        name="modulation",
    )(c_rows, w_mod, b_mod.reshape(depth, 1, d3))


def _proj_kernel(*refs, rope):
    if rope:
        x_ref, sh_ref, sc_ref, w_ref, cr_ref, sr_ref, ca_ref, sa_ref, o_ref, h_ref = refs
    else:
        x_ref, sh_ref, sc_ref, w_ref, o_ref, h_ref = refs
    j = pl.program_id(1)

    @pl.when(j == 0)
    def _():
        y = _layer_norm_rows(x_ref[...])
        h_ref[...] = (y * (1.0 + sc_ref[...]) + sh_ref[...]).astype(BF16)

    def product():
        return jnp.dot(h_ref[...], w_ref[...], preferred_element_type=F32)

    def store_rope_ret(scale):
        acc = product()
        c = cr_ref[...]
        s = sr_ref[...]
        for hh in range(RET_HEADS):
            lo = hh * RET_DK
            mid = lo + RET_DK // 2
            hi = lo + RET_DK
            x1 = acc[:, lo:mid]
            x2 = acc[:, mid:hi]
            o_ref[:, lo:mid] = ((x1 * c - x2 * s) * scale).astype(BF16)
            o_ref[:, mid:hi] = ((x1 * s + x2 * c) * scale).astype(BF16)

    def store_rope_att(scale):
        acc = product()
        c = ca_ref[...]
        s = sa_ref[...]
        lane = lax.broadcasted_iota(jnp.int32, c.shape, 1)
        first_half = (lane % ATT_DH) < (ATT_DH // 2)
        for hh in range(ATT_HEADS):
            lo = hh * LANES
            x = acc[:, lo:lo + LANES]
            partner = jnp.where(first_half,
                                pltpu.roll(x, LANES - ATT_DH // 2, axis=1),
                                pltpu.roll(x, ATT_DH // 2, axis=1))
            o_ref[:, lo:lo + LANES] = ((x * c + partner * s) * scale).astype(BF16)

    def store_scaled(scale):
        o_ref[...] = (product() * scale).astype(BF16)

    def store_plain():
        o_ref[...] = product().astype(BF16)

    def store_silu():
        a = product()
        o_ref[...] = (a * _sigmoid(a)).astype(BF16)

    def store_sigmoid():
        o_ref[...] = _sigmoid(product()).astype(BF16)

    q_r_scale = RET_DK ** -0.5
    q_a_scale = ATT_DH ** -0.5 * LOG2_E
    if rope:
        pl.when(j == G_RK)(functools.partial(store_rope_ret, 1.0))
        pl.when(j == G_RQ)(functools.partial(store_rope_ret, q_r_scale))
        pl.when(j == G_AK)(functools.partial(store_rope_att, 1.0))
        pl.when(j == G_AQ)(functools.partial(store_rope_att, q_a_scale))
        plain = (G_RV, G_AV, G_CB, G_CC, G_CX)
    else:
        pl.when(j == G_RQ)(functools.partial(store_scaled, q_r_scale))
        pl.when(j == G_AQ)(functools.partial(store_scaled, q_a_scale))
        plain = (G_RK, G_AK, G_RV, G_AV, G_CB, G_CC, G_CX)
    is_plain = functools.reduce(jnp.logical_or, [j == g for g in plain])
    pl.when(is_plain)(store_plain)
    is_silu = functools.reduce(jnp.logical_or, [j == g for g in (G_RG, G_AG, G_CG)])
    pl.when(is_silu)(store_silu)
    pl.when(j >= G_MR)(store_sigmoid)


def _projection(x2d, mod_rows, mod_row_of_tile, w_bf16, layer, n_groups, seq_len, rope_tabs):
    m, d = x2d.shape
    rope = rope_tabs is not None
    tm = min(PROJ_TM, seq_len if rope else m)
    tiles_per_seq = max(seq_len // tm, 1)
    in_specs = [
        pl.BlockSpec((tm, d), lambda i, j: (i, 0)),
        pl.BlockSpec((None, 1, d), lambda i, j: (mod_row_of_tile(i), 0, 0)),
        pl.BlockSpec((None, 1, d), lambda i, j: (mod_row_of_tile(i), 0, 1)),
        pl.BlockSpec((None, d, d), lambda i, j: (layer, 0, j)),
    ]
    args = [x2d, mod_rows, mod_rows, w_bf16]
    if rope:
        tab_spec = pl.BlockSpec((tm, LANES), lambda i, j: (i % tiles_per_seq, 0))
        in_specs += [tab_spec] * 4
        args += list(rope_tabs)
    return pl.pallas_call(
        functools.partial(_proj_kernel, rope=rope),
        grid=(m // tm, n_groups),
        in_specs=in_specs,
        out_specs=pl.BlockSpec((None, tm, d), lambda i, j: (j, i, 0)),
        out_shape=jax.ShapeDtypeStruct((n_groups, m, d), BF16),
        scratch_shapes=[pltpu.VMEM((tm, d), BF16)],
        compiler_params=_params("arbitrary", "arbitrary"),
        name="projection_rope" if rope else "projection_ctx",
    )(*args)


def _lanes(x, n):
    return x if n == LANES else jnp.concatenate([x] * (n // LANES), axis=1)


_KDEC_F, _KDEC_B, _QDEC_F, _QDEC_B = range(4)


def _ret_kernel(*refs, seq_len, chunk, zero_init, emit_out, emit_state):
    refs = list(refs)
    k_ref, v_ref = refs.pop(0), refs.pop(0)
    q_ref = g_ref = None
    if emit_out:
        q_ref, g_ref = refs.pop(0), refs.pop(0)
    df_ref, db_ref = refs.pop(0), refs.pop(0)
    sf0_ref = sb0_ref = None
    if not zero_init:
        sf0_ref, sb0_ref = refs.pop(0), refs.pop(0)
    y_ref = sfo_ref = sbo_ref = None
    if emit_out:
        y_ref = refs.pop(0)
    if emit_state:
        sfo_ref, sbo_ref = refs.pop(0), refs.pop(0)
    sf_ref, sb_ref, dec_ref, incf_ref, incb_ref = (refs.pop(0) for _ in range(5))
    dmat_ref = sfs_ref = sbs_ref = None
    if emit_out:
        dmat_ref, sfs_ref, sbs_ref = refs.pop(0), refs.pop(0), refs.pop(0)

    n_chunks = seq_len // chunk
    contract_rows = (((0,), (0,)), ((), ()))
    contract_last = (((1,), (1,)), ((), ()))

    lg_f = _lanes(-jnp.exp(df_ref[...]), RET_DK)
    lg_b = _lanes(-jnp.exp(db_ref[...]), RET_DK)
    pos = lax.broadcasted_iota(jnp.int32, (chunk, RET_DK), 0).astype(F32)
    dec_ref[_KDEC_F] = jnp.exp(lg_f * (chunk - 1.0 - pos))
    dec_ref[_KDEC_B] = jnp.exp(lg_b * pos)
    gc_f = jnp.exp(lg_f * float(chunk))
    gc_b = jnp.exp(lg_b * float(chunk))
    if emit_out:
        dec_ref[_QDEC_F] = jnp.exp(lg_f * (pos + 1.0))
        dec_ref[_QDEC_B] = jnp.exp(lg_b * (float(chunk) - pos))
        row = lax.broadcasted_iota(jnp.int32, (chunk, chunk), 0)
        col = lax.broadcasted_iota(jnp.int32, (chunk, chunk), 1)
        dist = (row - col).astype(F32)
        lgf_c = _lanes(lg_f[:, :LANES], chunk)
        lgb_c = _lanes(lg_b[:, :LANES], chunk)
        dmat_ref[...] = jnp.where(dist >= 0.0,
                                  jnp.exp(lgf_c * jnp.maximum(dist, 0.0)),
                                  jnp.exp(lgb_c * jnp.maximum(-dist, 0.0)))

    if zero_init:
        sf_ref[...] = jnp.zeros_like(sf_ref)
        sb_ref[...] = jnp.zeros_like(sb_ref)
    else:
        sf_ref[...] = sf0_ref[...]
        sb_ref[...] = sb0_ref[...]

    def rows(n):
        return slice(n * chunk, (n + 1) * chunk)

    for n in range(n_chunks):
        kn = k_ref[rows(n), :].astype(F32)
        vn = v_ref[rows(n), :]
        incf_ref[n] = lax.dot_general((kn * dec_ref[_KDEC_F]).astype(BF16), vn, contract_rows,
                                      preferred_element_type=F32)
        incb_ref[n] = lax.dot_general((kn * dec_ref[_KDEC_B]).astype(BF16), vn, contract_rows,
                                      preferred_element_type=F32)

    for n in range(n_chunks):
        if emit_out:
            sfs_ref[n] = sf_ref[...].astype(BF16)
        sf_ref[...] = gc_f * sf_ref[...] + incf_ref[n]
    for n in reversed(range(n_chunks)):
        if emit_out:
            sbs_ref[n] = sb_ref[...].astype(BF16)
        sb_ref[...] = gc_b * sb_ref[...] + incb_ref[n]

    if emit_out:
        for n in range(n_chunks):
            r = rows(n)
            qn, kn, vn = q_ref[r, :], k_ref[r, :], v_ref[r, :]
            scores = lax.dot_general(qn, kn, contract_last, preferred_element_type=F32) * dmat_ref[...]
            o = jnp.dot(scores.astype(BF16), vn, preferred_element_type=F32)
            o = o + jnp.dot(qn, sfs_ref[n], preferred_element_type=F32) * dec_ref[_QDEC_F]
            o = o + jnp.dot(qn, sbs_ref[n], preferred_element_type=F32) * dec_ref[_QDEC_B]
            y_ref[r, :] = (_layer_norm_rows(o) * g_ref[r, :].astype(F32)).astype(BF16)

    if emit_state:
        sfo_ref[...] = sf_ref[...]
        sbo_ref[...] = sb_ref[...]


def _retention(p, decay, batch, seq_len, states, emit_out, emit_state):
    m = batch * seq_len
    chunk = min(RET_CHUNK, seq_len)

    def group_spec(g):
        return pl.BlockSpec((None, seq_len, RET_DK), lambda b, h: (g, b, h))

    state_spec = pl.BlockSpec((None, None, RET_DK, RET_DV), lambda b, h: (b, h, 0, 0))
    in_specs = [group_spec(G_RK), group_spec(G_RV)]
    args = [p, p]
    if emit_out:
        in_specs += [group_spec(G_RQ), group_spec(G_RG)]
        args += [p, p]
    in_specs += [pl.BlockSpec((None, 1, LANES), lambda b, h: (h, 0, 0)),
                 pl.BlockSpec((None, 1, LANES), lambda b, h: (RET_HEADS + h, 0, 0))]
    args += [decay, decay]
    if states is not None:
        in_specs += [state_spec, state_spec]
        args += list(states)
    out_specs, out_shape = [], []
    if emit_out:
        out_specs.append(pl.BlockSpec((seq_len, RET_DV), lambda b, h: (b, h)))
        out_shape.append(jax.ShapeDtypeStruct((m, RET_HEADS * RET_DV), BF16))
    if emit_state:
        out_specs += [state_spec, state_spec]
        out_shape += [jax.ShapeDtypeStruct((batch, RET_HEADS, RET_DK, RET_DV), F32)] * 2
    n_chunks = seq_len // chunk
    state = (RET_DK, RET_DV)
    scratch = [pltpu.VMEM(state, F32), pltpu.VMEM(state, F32),
               pltpu.VMEM((4, chunk, RET_DK), F32),
               pltpu.VMEM((n_chunks,) + state, F32),
               pltpu.VMEM((n_chunks,) + state, F32)]
    if emit_out:
        scratch += [pltpu.VMEM((chunk, chunk), F32),
                    pltpu.VMEM((n_chunks,) + state, BF16),
                    pltpu.VMEM((n_chunks,) + state, BF16)]
    outs = pl.pallas_call(
        functools.partial(_ret_kernel, seq_len=seq_len, chunk=chunk, zero_init=states is None,
                          emit_out=emit_out, emit_state=emit_state),
        grid=(batch, RET_HEADS),
        in_specs=in_specs,
        out_specs=out_specs,
        out_shape=out_shape,
        scratch_shapes=scratch,
        compiler_params=_params("arbitrary", "arbitrary"),
        name="retention_%d" % seq_len,
    )(*args)
    outs = list(outs)
    y = outs.pop(0) if emit_out else None
    st = tuple(outs) if emit_state else None
    return y, st


def _att_kernel(*refs, n_lat, tk, lam_init):
    refs = list(refs)
    q_ref, kc_ref, vc_ref = refs.pop(0), refs.pop(0), refs.pop(0)
    kl_ref = vl_ref = None
    if n_lat:
        kl_ref, vl_ref = refs.pop(0), refs.pop(0)
    g_ref, lam_ref, o_ref = refs.pop(0), refs.pop(0), refs.pop(0)
    vt_ref, pt_ref, mc_ref, lc_ref, acc_ref = refs

    tq = q_ref.shape[0]
    t_ctx = kc_ref.shape[0]
    n_chunks = 1 + n_lat
    sub = 8
    contract_last = (((1,), (1,)), ((), ()))

    def span(c):
        return (0, t_ctx) if c == 0 else (t_ctx + (c - 1) * tk, t_ctx + c * tk)

    def keys(c):
        return kc_ref[...] if c == 0 else kl_ref[(c - 1) * tk:c * tk, :]

    @pl.when(pl.program_id(2) == 0)
    def _():
        vt_ref[:, 0:t_ctx] = vc_ref[...].astype(F32).T.astype(BF16)
        for c in range(1, n_chunks):
            lo, hi = span(c)
            vt_ref[:, lo:hi] = vl_ref[(c - 1) * tk:c * tk, :].astype(F32).T.astype(BF16)

    q = q_ref[...]
    lane = lax.broadcasted_iota(jnp.int32, q.shape, 1)
    zero = jnp.zeros_like(q)
    qm = (jnp.where(lane < ATT_DH, q, zero), jnp.where(lane >= ATT_DH, q, zero))

    kb, qb = ATT_STAT_KEYS, ATT_STAT_QUERIES
    for c in range(n_chunks):
        lo, hi = span(c)
        k = keys(c)
        for j in range(2):
            for q0 in range(0, tq, qb):
                st = lax.dot_general(k, qm[j][q0:q0 + qb, :], contract_last,
                                     preferred_element_type=F32)
                for r0 in range(0, hi - lo, kb):
                    blk = st[r0:r0 + kb, :].reshape(kb // sub, sub, qb)
                    m8 = jnp.max(blk, axis=0)
                    p = jnp.exp2(blk - m8[None])
                    b = (lo + r0) // kb
                    mc_ref[j, b, :, q0:q0 + qb] = m8
                    lc_ref[j, b, :, q0:q0 + qb] = jnp.sum(p, axis=0)
                    pt_ref[j, lo + r0:lo + r0 + kb, q0:q0 + qb] = p.reshape(kb, qb).astype(BF16)

    n_blocks = mc_ref.shape[1]
    denom = []
    for j in range(2):
        m = functools.reduce(jnp.maximum, [mc_ref[j, b] for b in range(n_blocks)])
        m = jnp.max(m, axis=0, keepdims=True)
        part = None
        for b in range(n_blocks):
            w = jnp.exp2(mc_ref[j, b] - m)
            mc_ref[j, b] = w
            part = w * lc_ref[j, b] if part is None else part + w * lc_ref[j, b]
        denom.append(jnp.sum(part, axis=0, keepdims=True))
    l0, l1 = denom

    lv = lam_ref[...]
    lam = (jnp.exp(jnp.sum(lv[0:1, :] * lv[1:2, :], axis=-1, keepdims=True))
           - jnp.exp(jnp.sum(lv[2:3, :] * lv[3:4, :], axis=-1, keepdims=True)) + lam_init)
    c1 = lam * l0 / l1

    pack = BF16_SUBLANES
    for c in range(n_chunks):
        lo, hi = span(c)
        parts = []
        for r0 in range(lo, hi, kb):
            w0 = mc_ref[0, r0 // kb]
            w1 = -c1 * mc_ref[1, r0 // kb]
            coef0 = jnp.concatenate([w0, w0], axis=0).astype(BF16)
            coef1 = jnp.concatenate([w1, w1], axis=0).astype(BF16)
            p0 = pt_ref[0, r0:r0 + kb, :].reshape(kb // pack, pack, tq)
            p1 = pt_ref[1, r0:r0 + kb, :].reshape(kb // pack, pack, tq)
            parts.append((p0 * coef0[None] + p1 * coef1[None]).reshape(kb, tq))
        a = parts[0] if len(parts) == 1 else jnp.concatenate(parts, axis=0)
        pv = jnp.dot(vt_ref[:, lo:hi], a, preferred_element_type=F32)
        if c == 0:
            acc_ref[...] = pv
        else:
            acc_ref[...] += pv

    ot = acc_ref[...] / l0
    ms = jnp.mean(ot * ot, axis=0, keepdims=True)
    ot = ot * lax.rsqrt(ms + LN_EPS) * (1.0 - lam_init)
    o_ref[...] = (ot.T * g_ref[...].astype(F32)).astype(BF16)


def _attention(p_q, p_ctx, p_lat, lam_rows, batch, q_len, ctx_len, lat_len, lam_init):
    m = batch * q_len
    tq = min(ATT_TQ, q_len)
    nq = q_len // tq
    tk = ATT_TK
    n_lat = 0 if p_lat is None else lat_len // tk
    n_keys = ctx_len + n_lat * tk
    width = 2 * ATT_DH

    def q_spec(g):
        return pl.BlockSpec((None, tq, width), lambda b, h, i: (g, b * nq + i, h))

    def kv_spec(g, length):
        return pl.BlockSpec((None, length, width), lambda b, h, i: (g, b, h))

    in_specs = [q_spec(G_AQ), kv_spec(G_AK, ctx_len), kv_spec(G_AV, ctx_len)]
    args = [p_q, p_ctx, p_ctx]
    if n_lat:
        in_specs += [kv_spec(G_AK, lat_len), kv_spec(G_AV, lat_len)]
        args += [p_lat, p_lat]
    in_specs += [q_spec(G_AG), pl.BlockSpec(lam_rows.shape, lambda b, h, i: (0, 0))]
    args += [p_q, lam_rows]
    n_chunks = 1 + n_lat
    scratch = [pltpu.VMEM((width, n_keys), BF16),
               pltpu.VMEM((2, n_keys, tq), BF16),
               pltpu.VMEM((2, n_keys // ATT_STAT_KEYS, 8, tq), F32),
               pltpu.VMEM((2, n_keys // ATT_STAT_KEYS, 8, tq), F32),
               pltpu.VMEM((width, tq), F32)]
    return pl.pallas_call(
        functools.partial(_att_kernel, n_lat=n_lat, tk=tk, lam_init=lam_init),
        grid=(batch, ATT_HEADS, nq),
        in_specs=in_specs,
        out_specs=pl.BlockSpec((tq, width), lambda b, h, i: (b * nq + i, h)),
        out_shape=jax.ShapeDtypeStruct((m, ATT_HEADS * width), BF16),
        scratch_shapes=scratch,
        compiler_params=_params("arbitrary", "arbitrary", "arbitrary"),
        name="diff_attention_%d" % q_len,
    )(*args)


def _merge_kernel(yr_ref, ya_ref, cb_ref, cc_ref, cx_ref, cg_ref, ccp_ref, cxp_ref, ccn_ref, cxn_ref,
                  gr_ref, gc_ref, ga_ref, x_ref, gate_ref, cw_ref, lng_ref, lnb_ref,
                  wr_ref, wc_ref, wa_ref, wo_ref, o_ref, *, tiles_per_seq):
    i = pl.program_id(0)
    tm = x_ref.shape[0]
    t = i % tiles_per_seq
    u = cc_ref[...].astype(F32) * cx_ref[...].astype(F32)
    halo = BF16_SUBLANES
    u_before = ccp_ref[halo - 1:halo, :].astype(F32) * cxp_ref[halo - 1:halo, :].astype(F32)
    u_after = ccn_ref[0:1, :].astype(F32) * cxn_ref[0:1, :].astype(F32)
    u_before = jnp.where(t == 0, 0.0, u_before)
    u_after = jnp.where(t == tiles_per_seq - 1, 0.0, u_after)
    row = lax.broadcasted_iota(jnp.int32, u.shape, 0)
    u_prev = jnp.where(row == 0, u_before, pltpu.roll(u, 1, axis=0))
    u_next = jnp.where(row == tm - 1, u_after, pltpu.roll(u, tm - 1, axis=0))
    cw = cw_ref[...]
    conv = cw[0:1, :] * u_prev + cw[1:2, :] * u + cw[2:3, :] * u_next
    y_conv = (cb_ref[...].astype(F32) * conv * cg_ref[...].astype(F32)).astype(BF16)

    n_parts = 2 if tm % (2 * BF16_SUBLANES) == 0 else 1
    for part in range(n_parts):
        r = slice(part * tm // n_parts, (part + 1) * tm // n_parts)
        mix = gr_ref[r, :].astype(F32) * jnp.dot(yr_ref[r, :], wr_ref[...], preferred_element_type=F32)
        mix = mix + gc_ref[r, :].astype(F32) * jnp.dot(y_conv[r, :], wc_ref[...], preferred_element_type=F32)
        mix = mix + ga_ref[r, :].astype(F32) * jnp.dot(ya_ref[r, :], wa_ref[...], preferred_element_type=F32)
        out = jnp.dot(mix.astype(BF16), wo_ref[...], preferred_element_type=F32)
        z = DEEPNORM_ALPHA * x_ref[r, :] + gate_ref[...] * out
        o_ref[r, :] = _layer_norm_rows(z) * lng_ref[...] + lnb_ref[...]


def _merge(p, y_ret, y_att, x2d, mod_rows, mod_row_of_tile, conv_w, ln_g, ln_b, w_r, w_c, w_a, w_o, seq_len):
    m, d = x2d.shape
    tm = min(MERGE_TM, seq_len)
    tiles_per_seq = seq_len // tm
    halo = BF16_SUBLANES
    hb = tm // halo
    n_halo = m // halo

    def group_spec(g):
        return pl.BlockSpec((None, tm, d), lambda i: (g, i, 0))

    def before_spec(g):
        return pl.BlockSpec((None, halo, d), lambda i: (g, jnp.maximum(i * hb - 1, 0), 0))

    def after_spec(g):
        return pl.BlockSpec((None, halo, d), lambda i: (g, jnp.minimum((i + 1) * hb, n_halo - 1), 0))

    row_spec = pl.BlockSpec((tm, d), lambda i: (i, 0))
    vec_spec = pl.BlockSpec((1, d), lambda i: (0, 0))
    w_spec = pl.BlockSpec((d, d), lambda i: (0, 0), pipeline_mode=pl.Buffered(1))
    in_specs = [row_spec, row_spec,
                group_spec(G_CB), group_spec(G_CC), group_spec(G_CX), group_spec(G_CG),
                before_spec(G_CC), before_spec(G_CX), after_spec(G_CC), after_spec(G_CX),
                group_spec(G_MR), group_spec(G_MC), group_spec(G_MA),
                row_spec,
                pl.BlockSpec((None, 1, d), lambda i: (mod_row_of_tile(i), 0, 2)),
                pl.BlockSpec((3, d), lambda i: (0, 0)), vec_spec, vec_spec,
                w_spec, w_spec, w_spec, w_spec]
    args = [y_ret, y_att, p, p, p, p, p, p, p, p, p, p, p, x2d, mod_rows, conv_w,
            ln_g.reshape(1, d), ln_b.reshape(1, d), w_r, w_c, w_a, w_o]
    return pl.pallas_call(
        functools.partial(_merge_kernel, tiles_per_seq=tiles_per_seq),
        grid=(m // tm,),
        in_specs=in_specs,
        out_specs=row_spec,
        out_shape=jax.ShapeDtypeStruct((m, d), F32),
        compiler_params=_params("arbitrary"),
        name="merge_%d" % seq_len,
    )(*args)


def _rope_tables(seq_len):
    pos = jnp.arange(seq_len, dtype=jnp.int32)
    rows = (pos // GRID_W).astype(F32)
    cols = (pos % GRID_W).astype(F32)

    def angles(head_dim):
        n_freq = head_dim // 4
        inv = ROPE_BASE ** (-jnp.arange(n_freq, dtype=F32) / n_freq)
        return jnp.concatenate([rows[:, None] * inv, cols[:, None] * inv], axis=-1)

    ang_r = angles(RET_DK)
    ang_a = angles(ATT_DH)
    cos_a, sin_a = jnp.cos(ang_a), jnp.sin(ang_a)
    cos_att = jnp.concatenate([cos_a] * 4, axis=-1)
    sin_att = jnp.concatenate([-sin_a, sin_a, -sin_a, sin_a], axis=-1)
    return jnp.cos(ang_r), jnp.sin(ang_r), cos_att, sin_att


def kernel(x, c, ctx, c_ctx, w_mod, b_mod, w_in, ret_decay, conv_w, diff_lambda,
           w_ret_out, w_conv_out, w_att_out, w_out, ln_g, ln_b):
    bsz, t_lat, d = x.shape
    t_ctx = ctx.shape[1]
    depth = w_mod.shape[0]
    ctx_row = bsz

    c_rows = jnp.concatenate([c, c_ctx[None, :], jnp.zeros((8 - bsz - 1, d), F32)], axis=0)
    mod = _modulation(c_rows, w_mod, b_mod)
    rope_tabs = _rope_tables(t_lat)
    w_in_bf = w_in.astype(BF16)
    w_r, w_c, w_a, w_o = (w.astype(BF16) for w in (w_ret_out, w_conv_out, w_att_out, w_out))
    decay = jnp.broadcast_to(ret_decay.astype(F32).reshape(depth, 2 * RET_HEADS, 1, 1),
                             (depth, 2 * RET_HEADS, 1, LANES))

    lat_tiles_proj = t_lat // min(PROJ_TM, t_lat)
    lat_tiles_merge = t_lat // min(MERGE_TM, t_lat)
    x2d = x.reshape(bsz * t_lat, d)
    xc2d = ctx.reshape(bsz * t_ctx, d)
    for l in range(depth):
        last = l == depth - 1
        lam_init = 0.8 - 0.6 * math.exp(-0.3 * l)
        mod_rows = mod[l].reshape(8, 1, 3 * d)
        lam_rows = diff_lambda[l].astype(F32)

        n_ctx_groups = N_KV_GROUPS if last else N_GROUPS
        pc = _projection(xc2d, mod_rows, lambda i: ctx_row, w_in_bf, l, n_ctx_groups, t_ctx, None)
        cy_ret, states = _retention(pc, decay[l], bsz, t_ctx, None, emit_out=not last, emit_state=True)

        p = _projection(x2d, mod_rows, lambda i: i // lat_tiles_proj, w_in_bf, l, N_GROUPS, t_lat, rope_tabs)
        y_ret, _ = _retention(p, decay[l], bsz, t_lat, states, emit_out=True, emit_state=False)
        y_att = _attention(p, pc, p, lam_rows, bsz, t_lat, t_ctx, t_lat, lam_init)
        x_new = _merge(p, y_ret, y_att, x2d, mod_rows, lambda i: i // lat_tiles_merge, conv_w[l],
                       ln_g[l], ln_b[l], w_r[l], w_c[l], w_a[l], w_o[l], t_lat)
        if not last:
            cy_att = _attention(pc, pc, None, lam_rows, bsz, t_ctx, t_ctx, 0, lam_init)
            xc2d = _merge(pc, cy_ret, cy_att, xc2d, mod_rows, lambda i: ctx_row, conv_w[l],
                          ln_g[l], ln_b[l], w_r[l], w_c[l], w_a[l], w_o[l], t_ctx)
        x2d = x_new
    return x2d.reshape(bsz, t_lat, d)
```

```python
import functools
import math

import jax
import jax.numpy as jnp
from jax import lax
from jax.experimental import pallas as pl
from jax.experimental.pallas import tpu as pltpu

F32 = jnp.float32
BF16 = jnp.bfloat16

D_MODEL = 1024
DEPTH = 4
GRID_W = 64
RET_HEADS = 4
RET_DK = 256
RET_DV = 256
ATT_HEADS = 8
ATT_DH = 64
ROPE_BASE = 10000.0
LN_EPS = 1e-6
DEEPNORM_ALPHA = (2 * DEPTH) ** 0.25
LOG2_E = math.log2(math.e)

G_RK, G_RV, G_AK, G_AV, G_RQ, G_RG, G_AQ, G_AG, G_CB, G_CC, G_CX, G_CG, G_MR, G_MC, G_MA = range(15)
N_GROUPS = 15
N_KV_GROUPS = 4

LANES = 128
BF16_SUBLANES = 16
VMEM_LIMIT = 56 * 1024 * 1024

RET_CHUNK = 256
PROJ_TM = 2048
ATT_TQ = 1024
ATT_TK = 256
ATT_STAT_KEYS = 128
ATT_STAT_QUERIES = 256
MERGE_TM = 512


def _sigmoid(x):
    return 0.5 * jnp.tanh(0.5 * x) + 0.5


def _layer_norm_rows(x):
    mu = jnp.mean(x, axis=-1, keepdims=True)
    xc = x - mu
    var = jnp.mean(xc * xc, axis=-1, keepdims=True)
    return xc * lax.rsqrt(var + LN_EPS)


def _params(*sem):
    return pltpu.CompilerParams(dimension_semantics=sem, vmem_limit_bytes=VMEM_LIMIT)


def _mod_kernel(c_ref, w_ref, b_ref, o_ref):
    c = c_ref[...]
    s = (c * _sigmoid(c)).astype(BF16)
    o_ref[...] = jnp.dot(s, w_ref[...].astype(BF16), preferred_element_type=F32) + b_ref[...]


def _modulation(c_rows, w_mod, b_mod):
    depth, d, d3 = w_mod.shape
    tn = 1024
    return pl.pallas_call(
        _mod_kernel,
        grid=(depth, d3 // tn),
        in_specs=[
            pl.BlockSpec((8, d), lambda l, j: (0, 0)),
            pl.BlockSpec((None, d, tn), lambda l, j: (l, 0, j)),
            pl.BlockSpec((None, 1, tn), lambda l, j: (l, 0, j)),
        ],
        out_specs=pl.BlockSpec((None, 8, tn), lambda l, j: (l, 0, j)),
        out_shape=jax.ShapeDtypeStruct((depth, 8, d3), F32),
        compiler_params=_params("arbitrary", "arbitrary"),
        name="modulation",
    )(c_rows, w_mod, b_mod.reshape(depth, 1, d3))


def _proj_kernel(*refs, rope):
    if rope:
        x_ref, sh_ref, sc_ref, w_ref, cr_ref, sr_ref, ca_ref, sa_ref, o_ref, h_ref = refs
    else:
        x_ref, sh_ref, sc_ref, w_ref, o_ref, h_ref = refs
    j = pl.program_id(1)

    @pl.when(j == 0)
    def _():
        y = _layer_norm_rows(x_ref[...])
        h_ref[...] = (y * (1.0 + sc_ref[...]) + sh_ref[...]).astype(BF16)

    def product():
        return jnp.dot(h_ref[...], w_ref[...].astype(BF16), preferred_element_type=F32)

    def store_rope_ret(scale):
        acc = product().astype(BF16)
        c = (cr_ref[...] * scale).astype(BF16)
        s = (sr_ref[...] * scale).astype(BF16)
        for hh in range(RET_HEADS):
            lo = hh * RET_DK
            mid = lo + RET_DK // 2
            hi = lo + RET_DK
            x1 = acc[:, lo:mid]
            x2 = acc[:, mid:hi]
            o_ref[:, lo:mid] = x1 * c - x2 * s
            o_ref[:, mid:hi] = x1 * s + x2 * c

    def store_rope_att(scale):
        acc = product()
        c = ca_ref[...]
        s = sa_ref[...]
        lane = lax.broadcasted_iota(jnp.int32, c.shape, 1)
        first_half = (lane % ATT_DH) < (ATT_DH // 2)
        for hh in range(ATT_HEADS):
            lo = hh * LANES
            x = acc[:, lo:lo + LANES]
            partner = jnp.where(first_half,
                                pltpu.roll(x, LANES - ATT_DH // 2, axis=1),
                                pltpu.roll(x, ATT_DH // 2, axis=1))
            o_ref[:, lo:lo + LANES] = ((x * c + partner * s) * scale).astype(BF16)

    def store_scaled(scale):
        o_ref[...] = (product() * scale).astype(BF16)

    def store_plain():
        o_ref[...] = product().astype(BF16)

    def store_silu():
        a = product().astype(BF16)
        o_ref[...] = a * _sigmoid(a)

    def store_sigmoid():
        o_ref[...] = _sigmoid(product().astype(BF16))

    q_r_scale = RET_DK ** -0.5
    q_a_scale = ATT_DH ** -0.5 * LOG2_E
    if rope:
        pl.when(j == G_RK)(functools.partial(store_rope_ret, 1.0))
        pl.when(j == G_RQ)(functools.partial(store_rope_ret, q_r_scale))
        pl.when(j == G_AK)(functools.partial(store_rope_att, 1.0))
        pl.when(j == G_AQ)(functools.partial(store_rope_att, q_a_scale))
        plain = (G_RV, G_AV, G_CB, G_CC, G_CX)
    else:
        pl.when(j == G_RQ)(functools.partial(store_scaled, q_r_scale))
        pl.when(j == G_AQ)(functools.partial(store_scaled, q_a_scale))
        plain = (G_RK, G_AK, G_RV, G_AV, G_CB, G_CC, G_CX)
    is_plain = functools.reduce(jnp.logical_or, [j == g for g in plain])
    pl.when(is_plain)(store_plain)
    is_silu = functools.reduce(jnp.logical_or, [j == g for g in (G_RG, G_AG, G_CG)])
    pl.when(is_silu)(store_silu)
    pl.when(j >= G_MR)(store_sigmoid)


def _projection(x2d, mod_rows, mod_row_of_tile, w_bf16, layer, n_groups, seq_len, rope_tabs):
    m, d = x2d.shape
    rope = rope_tabs is not None
    tm = min(PROJ_TM, seq_len if rope else m)
    tiles_per_seq = max(seq_len // tm, 1)
    in_specs = [
        pl.BlockSpec((tm, d), lambda i, j: (i, 0)),
        pl.BlockSpec((None, 1, d), lambda i, j: (mod_row_of_tile(i), 0, 0)),
        pl.BlockSpec((None, 1, d), lambda i, j: (mod_row_of_tile(i), 0, 1)),
        pl.BlockSpec((None, d, d), lambda i, j: (layer, 0, j)),
    ]
    args = [x2d, mod_rows, mod_rows, w_bf16]
    if rope:
        tab_spec = pl.BlockSpec((tm, LANES), lambda i, j: (i % tiles_per_seq, 0))
        in_specs += [tab_spec] * 4
        args += list(rope_tabs)
    return pl.pallas_call(
        functools.partial(_proj_kernel, rope=rope),
        grid=(m // tm, n_groups),
        in_specs=in_specs,
        out_specs=pl.BlockSpec((None, tm, d), lambda i, j: (j, i, 0)),
        out_shape=jax.ShapeDtypeStruct((n_groups, m, d), BF16),
        scratch_shapes=[pltpu.VMEM((tm, d), BF16)],
        compiler_params=_params("arbitrary", "arbitrary"),
        name="projection_rope" if rope else "projection_ctx",
    )(*args)


def _lanes(x, n):
    return x if n == LANES else jnp.concatenate([x] * (n // LANES), axis=1)


_KDEC_F, _KDEC_B, _QDEC_F, _QDEC_B = range(4)


def _ret_kernel(*refs, seq_len, chunk, zero_init, emit_out, emit_state):
    refs = list(refs)
    k_ref, v_ref = refs.pop(0), refs.pop(0)
    q_ref = g_ref = None
    if emit_out:
        q_ref, g_ref = refs.pop(0), refs.pop(0)
    df_ref, db_ref = refs.pop(0), refs.pop(0)
    sf0_ref = sb0_ref = None
    if not zero_init:
        sf0_ref, sb0_ref = refs.pop(0), refs.pop(0)
    y_ref = sfo_ref = sbo_ref = None
    if emit_out:
        y_ref = refs.pop(0)
    if emit_state:
        sfo_ref, sbo_ref = refs.pop(0), refs.pop(0)
    sf_ref, sb_ref, dec_ref, incf_ref, incb_ref = (refs.pop(0) for _ in range(5))
    dmat_ref = sfs_ref = sbs_ref = None
    if emit_out:
        dmat_ref, sfs_ref, sbs_ref = refs.pop(0), refs.pop(0), refs.pop(0)

    n_chunks = seq_len // chunk
    contract_rows = (((0,), (0,)), ((), ()))
    contract_last = (((1,), (1,)), ((), ()))

    lg_f = _lanes(-jnp.exp(df_ref[...]), RET_DK)
    lg_b = _lanes(-jnp.exp(db_ref[...]), RET_DK)
    pos = lax.broadcasted_iota(jnp.int32, (chunk, RET_DK), 0).astype(F32)
    dec_ref[_KDEC_F] = jnp.exp(lg_f * (chunk - 1.0 - pos))
    dec_ref[_KDEC_B] = jnp.exp(lg_b * pos)
    gc_f = jnp.exp(lg_f * float(chunk))
    gc_b = jnp.exp(lg_b * float(chunk))
    if emit_out:
        dec_ref[_QDEC_F] = jnp.exp(lg_f * (pos + 1.0))
        dec_ref[_QDEC_B] = jnp.exp(lg_b * (float(chunk) - pos))
        row = lax.broadcasted_iota(jnp.int32, (chunk, chunk), 0)
        col = lax.broadcasted_iota(jnp.int32, (chunk, chunk), 1)
        dist = (row - col).astype(F32)
        lgf_c = _lanes(lg_f[:, :LANES], chunk)
        lgb_c = _lanes(lg_b[:, :LANES], chunk)
        dmat_ref[...] = jnp.where(dist >= 0.0,
                                  jnp.exp(lgf_c * jnp.maximum(dist, 0.0)),
                                  jnp.exp(lgb_c * jnp.maximum(-dist, 0.0)))

    if zero_init:
        sf_ref[...] = jnp.zeros_like(sf_ref)
        sb_ref[...] = jnp.zeros_like(sb_ref)
    else:
        sf_ref[...] = sf0_ref[...]
        sb_ref[...] = sb0_ref[...]

    def rows(n):
        return slice(n * chunk, (n + 1) * chunk)

    for n in range(n_chunks):
        kn = k_ref[rows(n), :].astype(F32)
        vn = v_ref[rows(n), :]
        incf_ref[n] = lax.dot_general((kn * dec_ref[_KDEC_F]).astype(BF16), vn, contract_rows,
                                      preferred_element_type=F32)
        incb_ref[n] = lax.dot_general((kn * dec_ref[_KDEC_B]).astype(BF16), vn, contract_rows,
                                      preferred_element_type=F32)

    for n in range(n_chunks):
        if emit_out:
            sfs_ref[n] = sf_ref[...].astype(BF16)
        sf_ref[...] = gc_f * sf_ref[...] + incf_ref[n]
    for n in reversed(range(n_chunks)):
        if emit_out:
            sbs_ref[n] = sb_ref[...].astype(BF16)
        sb_ref[...] = gc_b * sb_ref[...] + incb_ref[n]

    if emit_out:
        for n in range(n_chunks):
            r = rows(n)
            qn, kn, vn = q_ref[r, :], k_ref[r, :], v_ref[r, :]
            scores = lax.dot_general(qn, kn, contract_last, preferred_element_type=F32) * dmat_ref[...]
            o = jnp.dot(scores.astype(BF16), vn, preferred_element_type=F32)
            o = o + jnp.dot(qn, sfs_ref[n], preferred_element_type=F32) * dec_ref[_QDEC_F]
            o = o + jnp.dot(qn, sbs_ref[n], preferred_element_type=F32) * dec_ref[_QDEC_B]
            y_ref[r, :] = (_layer_norm_rows(o) * g_ref[r, :].astype(F32)).astype(BF16)

    if emit_state:
        sfo_ref[...] = sf_ref[...]
        sbo_ref[...] = sb_ref[...]


def _retention(p, decay, batch, seq_len, states, emit_out, emit_state):
    m = batch * seq_len
    chunk = min(RET_CHUNK, seq_len)

    def group_spec(g):
        return pl.BlockSpec((None, seq_len, RET_DK), lambda b, h: (g, b, h))

    state_spec = pl.BlockSpec((None, None, RET_DK, RET_DV), lambda b, h: (b, h, 0, 0))
    in_specs = [group_spec(G_RK), group_spec(G_RV)]
    args = [p, p]
    if emit_out:
        in_specs += [group_spec(G_RQ), group_spec(G_RG)]
        args += [p, p]
    in_specs += [pl.BlockSpec((None, 1, LANES), lambda b, h: (h, 0, 0)),
                 pl.BlockSpec((None, 1, LANES), lambda b, h: (RET_HEADS + h, 0, 0))]
    args += [decay, decay]
    if states is not None:
        in_specs += [state_spec, state_spec]
        args += list(states)
    out_specs, out_shape = [], []
    if emit_out:
        out_specs.append(pl.BlockSpec((seq_len, RET_DV), lambda b, h: (b, h)))
        out_shape.append(jax.ShapeDtypeStruct((m, RET_HEADS * RET_DV), BF16))
    if emit_state:
        out_specs += [state_spec, state_spec]
        out_shape += [jax.ShapeDtypeStruct((batch, RET_HEADS, RET_DK, RET_DV), F32)] * 2
    n_chunks = seq_len // chunk
    state = (RET_DK, RET_DV)
    scratch = [pltpu.VMEM(state, F32), pltpu.VMEM(state, F32),
               pltpu.VMEM((4, chunk, RET_DK), F32),
               pltpu.VMEM((n_chunks,) + state, F32),
               pltpu.VMEM((n_chunks,) + state, F32)]
    if emit_out:
        scratch += [pltpu.VMEM((chunk, chunk), F32),
                    pltpu.VMEM((n_chunks,) + state, BF16),
                    pltpu.VMEM((n_chunks,) + state, BF16)]
    outs = pl.pallas_call(
        functools.partial(_ret_kernel, seq_len=seq_len, chunk=chunk, zero_init=states is None,
                          emit_out=emit_out, emit_state=emit_state),
        grid=(batch, RET_HEADS),
        in_specs=in_specs,
        out_specs=out_specs,
        out_shape=out_shape,
        scratch_shapes=scratch,
        compiler_params=_params("arbitrary", "arbitrary"),
        name="retention_%d" % seq_len,
    )(*args)
    outs = list(outs)
    y = outs.pop(0) if emit_out else None
    st = tuple(outs) if emit_state else None
    return y, st


def _att_kernel(*refs, n_lat, tk, lam_init):
    refs = list(refs)
    q_ref, kc_ref, vc_ref = refs.pop(0), refs.pop(0), refs.pop(0)
    kl_ref = vl_ref = None
    if n_lat:
        kl_ref, vl_ref = refs.pop(0), refs.pop(0)
    g_ref, lam_ref, o_ref = refs.pop(0), refs.pop(0), refs.pop(0)
    vt_ref, pt_ref, mc_ref, lc_ref, acc_ref = refs

    tq = q_ref.shape[0]
    t_ctx = kc_ref.shape[0]
    n_chunks = 1 + n_lat
    sub = 8
    contract_last = (((1,), (1,)), ((), ()))

    def span(c):
        return (0, t_ctx) if c == 0 else (t_ctx + (c - 1) * tk, t_ctx + c * tk)

    def keys(c):
        return kc_ref[...] if c == 0 else kl_ref[(c - 1) * tk:c * tk, :]

    @pl.when(pl.program_id(2) == 0)
    def _():
        vt_ref[:, 0:t_ctx] = vc_ref[...].astype(F32).T.astype(BF16)
        for c in range(1, n_chunks):
            lo, hi = span(c)
            vt_ref[:, lo:hi] = vl_ref[(c - 1) * tk:c * tk, :].astype(F32).T.astype(BF16)

    q = q_ref[...]
    lane = lax.broadcasted_iota(jnp.int32, q.shape, 1)
    zero = jnp.zeros_like(q)
    qm = (jnp.where(lane < ATT_DH, q, zero), jnp.where(lane >= ATT_DH, q, zero))

    kb, qb = ATT_STAT_KEYS, ATT_STAT_QUERIES
    for c in range(n_chunks):
        lo, hi = span(c)
        k = keys(c)
        for j in range(2):
            for q0 in range(0, tq, qb):
                st = lax.dot_general(k, qm[j][q0:q0 + qb, :], contract_last,
                                     preferred_element_type=F32)
                for r0 in range(0, hi - lo, kb):
                    blk = st[r0:r0 + kb, :].reshape(kb // sub, sub, qb)
                    m8 = jnp.max(blk, axis=0)
                    p = jnp.exp2(blk - m8[None])
                    b = (lo + r0) // kb
                    mc_ref[j, b, :, q0:q0 + qb] = m8
                    lc_ref[j, b, :, q0:q0 + qb] = jnp.sum(p, axis=0)
                    pt_ref[j, lo + r0:lo + r0 + kb, q0:q0 + qb] = p.reshape(kb, qb).astype(BF16)

    n_blocks = mc_ref.shape[1]
    denom = []
    for j in range(2):
        m = functools.reduce(jnp.maximum, [mc_ref[j, b] for b in range(n_blocks)])
        m = jnp.max(m, axis=0, keepdims=True)
        part = None
        for b in range(n_blocks):
            w = jnp.exp2(mc_ref[j, b] - m)
            mc_ref[j, b] = w
            part = w * lc_ref[j, b] if part is None else part + w * lc_ref[j, b]
        denom.append(jnp.sum(part, axis=0, keepdims=True))
    l0, l1 = denom

    lv = lam_ref[...]
    lam = (jnp.exp(jnp.sum(lv[0:1, :] * lv[1:2, :], axis=-1, keepdims=True))
           - jnp.exp(jnp.sum(lv[2:3, :] * lv[3:4, :], axis=-1, keepdims=True)) + lam_init)
    c1 = lam * l0 / l1

    pack = BF16_SUBLANES
    for c in range(n_chunks):
        lo, hi = span(c)
        parts = []
        for r0 in range(lo, hi, kb):
            w0 = mc_ref[0, r0 // kb]
            w1 = -c1 * mc_ref[1, r0 // kb]
            coef0 = jnp.concatenate([w0, w0], axis=0).astype(BF16)
            coef1 = jnp.concatenate([w1, w1], axis=0).astype(BF16)
            p0 = pt_ref[0, r0:r0 + kb, :].reshape(kb // pack, pack, tq)
            p1 = pt_ref[1, r0:r0 + kb, :].reshape(kb // pack, pack, tq)
            parts.append((p0 * coef0[None] + p1 * coef1[None]).reshape(kb, tq))
        a = parts[0] if len(parts) == 1 else jnp.concatenate(parts, axis=0)
        pv = jnp.dot(vt_ref[:, lo:hi], a, preferred_element_type=F32)
        if c == 0:
            acc_ref[...] = pv
        else:
            acc_ref[...] += pv

    ot = acc_ref[...] / l0
    ms = jnp.mean(ot * ot, axis=0, keepdims=True)
    ot = ot * lax.rsqrt(ms + LN_EPS) * (1.0 - lam_init)
    o_ref[...] = (ot.T * g_ref[...].astype(F32)).astype(BF16)


def _attention(p_q, p_ctx, p_lat, lam_rows, batch, q_len, ctx_len, lat_len, lam_init):
    m = batch * q_len
    tq = min(ATT_TQ, q_len)
    nq = q_len // tq
    tk = ATT_TK
    n_lat = 0 if p_lat is None else lat_len // tk
    n_keys = ctx_len + n_lat * tk
    width = 2 * ATT_DH

    def q_spec(g):
        return pl.BlockSpec((None, tq, width), lambda b, h, i: (g, b * nq + i, h))

    def kv_spec(g, length):
        return pl.BlockSpec((None, length, width), lambda b, h, i: (g, b, h))

    in_specs = [q_spec(G_AQ), kv_spec(G_AK, ctx_len), kv_spec(G_AV, ctx_len)]
    args = [p_q, p_ctx, p_ctx]
    if n_lat:
        in_specs += [kv_spec(G_AK, lat_len), kv_spec(G_AV, lat_len)]
        args += [p_lat, p_lat]
    in_specs += [q_spec(G_AG), pl.BlockSpec(lam_rows.shape, lambda b, h, i: (0, 0))]
    args += [p_q, lam_rows]
    n_chunks = 1 + n_lat
    scratch = [pltpu.VMEM((width, n_keys), BF16),
               pltpu.VMEM((2, n_keys, tq), BF16),
               pltpu.VMEM((2, n_keys // ATT_STAT_KEYS, 8, tq), F32),
               pltpu.VMEM((2, n_keys // ATT_STAT_KEYS, 8, tq), F32),
               pltpu.VMEM((width, tq), F32)]
    return pl.pallas_call(
        functools.partial(_att_kernel, n_lat=n_lat, tk=tk, lam_init=lam_init),
        grid=(batch, ATT_HEADS, nq),
        in_specs=in_specs,
        out_specs=pl.BlockSpec((tq, width), lambda b, h, i: (b * nq + i, h)),
        out_shape=jax.ShapeDtypeStruct((m, ATT_HEADS * width), BF16),
        scratch_shapes=scratch,
        compiler_params=_params("arbitrary", "arbitrary", "arbitrary"),
        name="diff_attention_%d" % q_len,
    )(*args)


def _merge_kernel(yr_ref, ya_ref, cb_ref, cc_ref, cx_ref, cg_ref, ccp_ref, cxp_ref, ccn_ref, cxn_ref,
                  gr_ref, gc_ref, ga_ref, x_ref, gate_ref, cw_ref, lng_ref, lnb_ref,
                  wr_ref, wc_ref, wa_ref, wo_ref, o_ref, *, tiles_per_seq):
    i = pl.program_id(0)
    tm = x_ref.shape[0]
    t = i % tiles_per_seq
    u = cc_ref[...].astype(F32) * cx_ref[...].astype(F32)
    halo = BF16_SUBLANES
    u_before = ccp_ref[halo - 1:halo, :].astype(F32) * cxp_ref[halo - 1:halo, :].astype(F32)
    u_after = ccn_ref[0:1, :].astype(F32) * cxn_ref[0:1, :].astype(F32)
    u_before = jnp.where(t == 0, 0.0, u_before)
    u_after = jnp.where(t == tiles_per_seq - 1, 0.0, u_after)
    row = lax.broadcasted_iota(jnp.int32, u.shape, 0)
    u_prev = jnp.where(row == 0, u_before, pltpu.roll(u, 1, axis=0))
    u_next = jnp.where(row == tm - 1, u_after, pltpu.roll(u, tm - 1, axis=0))
    cw = cw_ref[...]
    conv = cw[0:1, :] * u_prev + cw[1:2, :] * u + cw[2:3, :] * u_next
    y_conv = (cb_ref[...].astype(F32) * conv * cg_ref[...].astype(F32)).astype(BF16)

    n_parts = 2 if tm % (2 * BF16_SUBLANES) == 0 else 1
    for part in range(n_parts):
        r = slice(part * tm // n_parts, (part + 1) * tm // n_parts)
        mix = gr_ref[r, :].astype(F32) * jnp.dot(yr_ref[r, :], wr_ref[...], preferred_element_type=F32)
        mix = mix + gc_ref[r, :].astype(F32) * jnp.dot(y_conv[r, :], wc_ref[...], preferred_element_type=F32)
        mix = mix + ga_ref[r, :].astype(F32) * jnp.dot(ya_ref[r, :], wa_ref[...], preferred_element_type=F32)
        out = jnp.dot(mix.astype(BF16), wo_ref[...], preferred_element_type=F32)
        z = DEEPNORM_ALPHA * x_ref[r, :] + gate_ref[...] * out
        o_ref[r, :] = _layer_norm_rows(z) * lng_ref[...] + lnb_ref[...]


def _merge(p, y_ret, y_att, x2d, mod_rows, mod_row_of_tile, conv_w, ln_g, ln_b, w_r, w_c, w_a, w_o, seq_len):
    m, d = x2d.shape
    tm = min(MERGE_TM, seq_len)
    tiles_per_seq = seq_len // tm
    halo = BF16_SUBLANES
    hb = tm // halo
    n_halo = m // halo

    def group_spec(g):
        return pl.BlockSpec((None, tm, d), lambda i: (g, i, 0))

    def before_spec(g):
        return pl.BlockSpec((None, halo, d), lambda i: (g, jnp.maximum(i * hb - 1, 0), 0))

    def after_spec(g):
        return pl.BlockSpec((None, halo, d), lambda i: (g, jnp.minimum((i + 1) * hb, n_halo - 1), 0))

    row_spec = pl.BlockSpec((tm, d), lambda i: (i, 0))
    vec_spec = pl.BlockSpec((1, d), lambda i: (0, 0))
    w_spec = pl.BlockSpec((d, d), lambda i: (0, 0), pipeline_mode=pl.Buffered(1))
    in_specs = [row_spec, row_spec,
                group_spec(G_CB), group_spec(G_CC), group_spec(G_CX), group_spec(G_CG),
                before_spec(G_CC), before_spec(G_CX), after_spec(G_CC), after_spec(G_CX),
                group_spec(G_MR), group_spec(G_MC), group_spec(G_MA),
                row_spec,
                pl.BlockSpec((None, 1, d), lambda i: (mod_row_of_tile(i), 0, 2)),
                pl.BlockSpec((3, d), lambda i: (0, 0)), vec_spec, vec_spec,
                w_spec, w_spec, w_spec, w_spec]
    args = [y_ret, y_att, p, p, p, p, p, p, p, p, p, p, p, x2d, mod_rows, conv_w,
            ln_g.reshape(1, d), ln_b.reshape(1, d), w_r, w_c, w_a, w_o]
    return pl.pallas_call(
        functools.partial(_merge_kernel, tiles_per_seq=tiles_per_seq),
        grid=(m // tm,),
        in_specs=in_specs,
        out_specs=row_spec,
        out_shape=jax.ShapeDtypeStruct((m, d), F32),
        compiler_params=_params("arbitrary"),
        name="merge_%d" % seq_len,
    )(*args)


def _rope_tables(seq_len):
    pos = jnp.arange(seq_len, dtype=jnp.int32)
    rows = (pos // GRID_W).astype(F32)
    cols = (pos % GRID_W).astype(F32)

    def angles(head_dim):
        n_freq = head_dim // 4
        inv = ROPE_BASE ** (-jnp.arange(n_freq, dtype=F32) / n_freq)
        return jnp.concatenate([rows[:, None] * inv, cols[:, None] * inv], axis=-1)

    ang_r = angles(RET_DK)
    ang_a = angles(ATT_DH)
    cos_a, sin_a = jnp.cos(ang_a), jnp.sin(ang_a)
    cos_att = jnp.concatenate([cos_a] * 4, axis=-1)
    sin_att = jnp.concatenate([-sin_a, sin_a, -sin_a, sin_a], axis=-1)
    return jnp.cos(ang_r), jnp.sin(ang_r), cos_att, sin_att


def kernel(x, c, ctx, c_ctx, w_mod, b_mod, w_in, ret_decay, conv_w, diff_lambda,
           w_ret_out, w_conv_out, w_att_out, w_out, ln_g, ln_b):
    bsz, t_lat, d = x.shape
    t_ctx = ctx.shape[1]
    depth = w_mod.shape[0]
    ctx_row = bsz

    c_rows = jnp.concatenate([c, c_ctx[None, :], jnp.zeros((8 - bsz - 1, d), F32)], axis=0)
    mod = _modulation(c_rows, w_mod, b_mod)
    rope_tabs = _rope_tables(t_lat)
    w_in_bf = w_in
    w_r, w_c, w_a, w_o = (w.astype(BF16) for w in (w_ret_out, w_conv_out, w_att_out, w_out))
    decay = jnp.broadcast_to(ret_decay.astype(F32).reshape(depth, 2 * RET_HEADS, 1, 1),
                             (depth, 2 * RET_HEADS, 1, LANES))

    lat_tiles_proj = t_lat // min(PROJ_TM, t_lat)
    lat_tiles_merge = t_lat // min(MERGE_TM, t_lat)
    x2d = x.reshape(bsz * t_lat, d)
    xc2d = ctx.reshape(bsz * t_ctx, d)
    for l in range(depth):
        last = l == depth - 1
        lam_init = 0.8 - 0.6 * math.exp(-0.3 * l)
        mod_rows = mod[l].reshape(8, 1, 3 * d)
        lam_rows = diff_lambda[l].astype(F32)

        n_ctx_groups = N_KV_GROUPS if last else N_GROUPS
        pc = _projection(xc2d, mod_rows, lambda i: ctx_row, w_in_bf, l, n_ctx_groups, t_ctx, None)
        cy_ret, states = _retention(pc, decay[l], bsz, t_ctx, None, emit_out=not last, emit_state=True)

        p = _projection(x2d, mod_rows, lambda i: i // lat_tiles_proj, w_in_bf, l, N_GROUPS, t_lat, rope_tabs)
        y_ret, _ = _retention(p, decay[l], bsz, t_lat, states, emit_out=True, emit_state=False)
        y_att = _attention(p, pc, p, lam_rows, bsz, t_lat, t_ctx, t_lat, lam_init)
        x_new = _merge(p, y_ret, y_att, x2d, mod_rows, lambda i: i // lat_tiles_merge, conv_w[l],
                       ln_g[l], ln_b[l], w_r[l], w_c[l], w_a[l], w_o[l], t_lat)
        if not last:
            cy_att = _attention(pc, pc, None, lam_rows, bsz, t_ctx, t_ctx, 0, lam_init)
            xc2d = _merge(pc, cy_ret, cy_att, xc2d, mod_rows, lambda i: ctx_row, conv_w[l],
                          ln_g[l], ln_b[l], w_r[l], w_c[l], w_a[l], w_o[l], t_ctx)
        x2d = x_new
    return x2d.reshape(bsz, t_lat, d)
```

```python
import functools
import math

import jax
import jax.numpy as jnp
from jax import lax
from jax.experimental import pallas as pl
from jax.experimental.pallas import tpu as pltpu

F32 = jnp.float32
BF16 = jnp.bfloat16

D_MODEL = 1024
DEPTH = 4
GRID_W = 64
RET_HEADS = 4
RET_DK = 256
RET_DV = 256
ATT_HEADS = 8
ATT_DH = 64
ROPE_BASE = 10000.0
LN_EPS = 1e-6
DEEPNORM_ALPHA = (2 * DEPTH) ** 0.25
LOG2_E = math.log2(math.e)

G_RK, G_RV, G_AK, G_AV, G_RQ, G_RG, G_AQ, G_AG, G_CB, G_CC, G_CX, G_CG, G_MR, G_MC, G_MA = range(15)
N_GROUPS = 15
N_KV_GROUPS = 4

LANES = 128
BF16_SUBLANES = 16
VMEM_LIMIT = 56 * 1024 * 1024

RET_CHUNK = 256
PROJ_TM = 2048
LN_TM = 1024
ATT_TQ = 1024
ATT_TK = 256
ATT_STAT_KEYS = 128
ATT_STAT_QUERIES = 256
MERGE_TM = 512


def _sigmoid(x):
    return 0.5 * jnp.tanh(0.5 * x) + 0.5


def _layer_norm_rows(x):
    mu = jnp.mean(x, axis=-1, keepdims=True)
    xc = x - mu
    var = jnp.mean(xc * xc, axis=-1, keepdims=True)
    return xc * lax.rsqrt(var + LN_EPS)


def _params(*sem):
    return pltpu.CompilerParams(dimension_semantics=sem, vmem_limit_bytes=VMEM_LIMIT)


def _mod_kernel(c_ref, w_ref, b_ref, o_ref):
    c = c_ref[...]
    s = (c * _sigmoid(c)).astype(BF16)
    o_ref[...] = jnp.dot(s, w_ref[...].astype(BF16), preferred_element_type=F32) + b_ref[...]


def _modulation(c_rows, w_mod, b_mod):
    depth, d, d3 = w_mod.shape
    tn = 1024
    return pl.pallas_call(
        _mod_kernel,
        grid=(depth, d3 // tn),
        in_specs=[
            pl.BlockSpec((8, d), lambda l, j: (0, 0)),
            pl.BlockSpec((None, d, tn), lambda l, j: (l, 0, j)),
            pl.BlockSpec((None, 1, tn), lambda l, j: (l, 0, j)),
        ],
        out_specs=pl.BlockSpec((None, 8, tn), lambda l, j: (l, 0, j)),
        out_shape=jax.ShapeDtypeStruct((depth, 8, d3), F32),
        compiler_params=_params("arbitrary", "arbitrary"),
        name="modulation",
    )(c_rows, w_mod, b_mod.reshape(depth, 1, d3))


def _ln_mod_kernel(x_ref, sh_ref, sc_ref, o_ref):
    y = _layer_norm_rows(x_ref[...])
    o_ref[...] = (y * (1.0 + sc_ref[...]) + sh_ref[...]).astype(BF16)


def _ln_modulate(x2d, mod_rows, mod_row_of_tile, tm):
    m, d = x2d.shape
    return pl.pallas_call(
        _ln_mod_kernel,
        grid=(m // tm,),
        in_specs=[pl.BlockSpec((tm, d), lambda i: (i, 0)),
                  pl.BlockSpec((None, 1, d), lambda i: (mod_row_of_tile(i), 0, 0)),
                  pl.BlockSpec((None, 1, d), lambda i: (mod_row_of_tile(i), 0, 1))],
        out_specs=pl.BlockSpec((tm, d), lambda i: (i, 0)),
        out_shape=jax.ShapeDtypeStruct((m, d), BF16),
        compiler_params=_params("arbitrary"),
        name="ln_modulate",
    )(x2d, mod_rows, mod_rows)


def _proj_kernel(*refs, rope):
    if rope:
        h_ref, w_ref, cr_ref, sr_ref, ca_ref, sa_ref, o_ref = refs
    else:
        h_ref, w_ref, o_ref = refs
    j = pl.program_id(0)

    def product():
        return jnp.dot(h_ref[...], w_ref[...].astype(BF16), preferred_element_type=F32)

    def store_rope_ret(scale):
        acc = product().astype(BF16)
        c = (cr_ref[...] * scale).astype(BF16)
        s = (sr_ref[...] * scale).astype(BF16)
        for hh in range(RET_HEADS):
            lo = hh * RET_DK
            mid = lo + RET_DK // 2
            hi = lo + RET_DK
            x1 = acc[:, lo:mid]
            x2 = acc[:, mid:hi]
            o_ref[:, lo:mid] = x1 * c - x2 * s
            o_ref[:, mid:hi] = x1 * s + x2 * c

    def store_rope_att(scale):
        acc = product()
        c = ca_ref[...]
        s = sa_ref[...]
        lane = lax.broadcasted_iota(jnp.int32, c.shape, 1)
        first_half = (lane % ATT_DH) < (ATT_DH // 2)
        for hh in range(ATT_HEADS):
            lo = hh * LANES
            x = acc[:, lo:lo + LANES]
            partner = jnp.where(first_half,
                                pltpu.roll(x, LANES - ATT_DH // 2, axis=1),
                                pltpu.roll(x, ATT_DH // 2, axis=1))
            o_ref[:, lo:lo + LANES] = ((x * c + partner * s) * scale).astype(BF16)

    def store_scaled(scale):
        o_ref[...] = (product() * scale).astype(BF16)

    def store_plain():
        o_ref[...] = product().astype(BF16)

    def store_silu():
        a = product().astype(BF16)
        o_ref[...] = a * _sigmoid(a)

    def store_sigmoid():
        o_ref[...] = _sigmoid(product().astype(BF16))

    q_r_scale = RET_DK ** -0.5
    q_a_scale = ATT_DH ** -0.5 * LOG2_E
    if rope:
        pl.when(j == G_RK)(functools.partial(store_rope_ret, 1.0))
        pl.when(j == G_RQ)(functools.partial(store_rope_ret, q_r_scale))
        pl.when(j == G_AK)(functools.partial(store_rope_att, 1.0))
        pl.when(j == G_AQ)(functools.partial(store_rope_att, q_a_scale))
        plain = (G_RV, G_AV, G_CB, G_CC, G_CX)
    else:
        pl.when(j == G_RQ)(functools.partial(store_scaled, q_r_scale))
        pl.when(j == G_AQ)(functools.partial(store_scaled, q_a_scale))
        plain = (G_RK, G_AK, G_RV, G_AV, G_CB, G_CC, G_CX)
    is_plain = functools.reduce(jnp.logical_or, [j == g for g in plain])
    pl.when(is_plain)(store_plain)
    is_silu = functools.reduce(jnp.logical_or, [j == g for g in (G_RG, G_AG, G_CG)])
    pl.when(is_silu)(store_silu)
    pl.when(j >= G_MR)(store_sigmoid)


def _projection(h2d, w_stack, layer, n_groups, seq_len, rope_tabs):
    m, d = h2d.shape
    rope = rope_tabs is not None
    tm = min(PROJ_TM, seq_len if rope else m)
    tiles_per_seq = max(seq_len // tm, 1)
    in_specs = [
        pl.BlockSpec((tm, d), lambda j, i: (i, 0)),
        pl.BlockSpec((None, d, d), lambda j, i: (layer, 0, j)),
    ]
    args = [h2d, w_stack]
    if rope:
        tab_spec = pl.BlockSpec((tm, LANES), lambda j, i: (i % tiles_per_seq, 0))
        in_specs += [tab_spec] * 4
        args += list(rope_tabs)
    return pl.pallas_call(
        functools.partial(_proj_kernel, rope=rope),
        grid=(n_groups, m // tm),
        in_specs=in_specs,
        out_specs=pl.BlockSpec((None, tm, d), lambda j, i: (j, i, 0)),
        out_shape=jax.ShapeDtypeStruct((n_groups, m, d), BF16),
        compiler_params=_params("arbitrary", "arbitrary"),
        name="projection_rope" if rope else "projection_ctx",
    )(*args)


def _lanes(x, n):
    return x if n == LANES else jnp.concatenate([x] * (n // LANES), axis=1)


_KDEC_F, _KDEC_B, _QDEC_F, _QDEC_B = range(4)


def _ret_kernel(*refs, seq_len, chunk, zero_init, emit_out, emit_state):
    refs = list(refs)
    k_ref, v_ref = refs.pop(0), refs.pop(0)
    q_ref = g_ref = None
    if emit_out:
        q_ref, g_ref = refs.pop(0), refs.pop(0)
    df_ref, db_ref = refs.pop(0), refs.pop(0)
    sf0_ref = sb0_ref = None
    if not zero_init:
        sf0_ref, sb0_ref = refs.pop(0), refs.pop(0)
    y_ref = sfo_ref = sbo_ref = None
    if emit_out:
        y_ref = refs.pop(0)
    if emit_state:
        sfo_ref, sbo_ref = refs.pop(0), refs.pop(0)
    sf_ref, sb_ref, dec_ref, incf_ref, incb_ref = (refs.pop(0) for _ in range(5))
    dmat_ref = sfs_ref = sbs_ref = None
    if emit_out:
        dmat_ref, sfs_ref, sbs_ref = refs.pop(0), refs.pop(0), refs.pop(0)

    n_chunks = seq_len // chunk
    contract_rows = (((0,), (0,)), ((), ()))
    contract_last = (((1,), (1,)), ((), ()))

    lg_f = _lanes(-jnp.exp(df_ref[...]), RET_DK)
    lg_b = _lanes(-jnp.exp(db_ref[...]), RET_DK)
    pos = lax.broadcasted_iota(jnp.int32, (chunk, RET_DK), 0).astype(F32)
    dec_ref[_KDEC_F] = jnp.exp(lg_f * (chunk - 1.0 - pos))
    dec_ref[_KDEC_B] = jnp.exp(lg_b * pos)
    gc_f = jnp.exp(lg_f * float(chunk))
    gc_b = jnp.exp(lg_b * float(chunk))
    if emit_out:
        dec_ref[_QDEC_F] = jnp.exp(lg_f * (pos + 1.0))
        dec_ref[_QDEC_B] = jnp.exp(lg_b * (float(chunk) - pos))
        row = lax.broadcasted_iota(jnp.int32, (chunk, chunk), 0)
        col = lax.broadcasted_iota(jnp.int32, (chunk, chunk), 1)
        dist = (row - col).astype(F32)
        lgf_c = _lanes(lg_f[:, :LANES], chunk)
        lgb_c = _lanes(lg_b[:, :LANES], chunk)
        dmat_ref[...] = jnp.where(dist >= 0.0,
                                  jnp.exp(lgf_c * jnp.maximum(dist, 0.0)),
                                  jnp.exp(lgb_c * jnp.maximum(-dist, 0.0)))

    if zero_init:
        sf_ref[...] = jnp.zeros_like(sf_ref)
        sb_ref[...] = jnp.zeros_like(sb_ref)
    else:
        sf_ref[...] = sf0_ref[...]
        sb_ref[...] = sb0_ref[...]

    def rows(n):
        return slice(n * chunk, (n + 1) * chunk)

    for n in range(n_chunks):
        kn = k_ref[rows(n), :].astype(F32)
        vn = v_ref[rows(n), :]
        incf_ref[n] = lax.dot_general((kn * dec_ref[_KDEC_F]).astype(BF16), vn, contract_rows,
                                      preferred_element_type=F32)
        incb_ref[n] = lax.dot_general((kn * dec_ref[_KDEC_B]).astype(BF16), vn, contract_rows,
                                      preferred_element_type=F32)

    for n in range(n_chunks):
        if emit_out:
            sfs_ref[n] = sf_ref[...].astype(BF16)
        sf_ref[...] = gc_f * sf_ref[...] + incf_ref[n]
    for n in reversed(range(n_chunks)):
        if emit_out:
            sbs_ref[n] = sb_ref[...].astype(BF16)
        sb_ref[...] = gc_b * sb_ref[...] + incb_ref[n]

    if emit_out:
        for n in range(n_chunks):
            r = rows(n)
            qn, kn, vn = q_ref[r, :], k_ref[r, :], v_ref[r, :]
            scores = lax.dot_general(qn, kn, contract_last, preferred_element_type=F32) * dmat_ref[...]
            o = jnp.dot(scores.astype(BF16), vn, preferred_element_type=F32)
            o = o + jnp.dot(qn, sfs_ref[n], preferred_element_type=F32) * dec_ref[_QDEC_F]
            o = o + jnp.dot(qn, sbs_ref[n], preferred_element_type=F32) * dec_ref[_QDEC_B]
            y_ref[r, :] = (_layer_norm_rows(o) * g_ref[r, :].astype(F32)).astype(BF16)

    if emit_state:
        sfo_ref[...] = sf_ref[...]
        sbo_ref[...] = sb_ref[...]


def _retention(p, decay, batch, seq_len, states, emit_out, emit_state):
    m = batch * seq_len
    chunk = min(RET_CHUNK, seq_len)

    def group_spec(g):
        return pl.BlockSpec((None, seq_len, RET_DK), lambda b, h: (g, b, h))

    state_spec = pl.BlockSpec((None, None, RET_DK, RET_DV), lambda b, h: (b, h, 0, 0))
    in_specs = [group_spec(G_RK), group_spec(G_RV)]
    args = [p, p]
    if emit_out:
        in_specs += [group_spec(G_RQ), group_spec(G_RG)]
        args += [p, p]
    in_specs += [pl.BlockSpec((None, 1, LANES), lambda b, h: (h, 0, 0)),
                 pl.BlockSpec((None, 1, LANES), lambda b, h: (RET_HEADS + h, 0, 0))]
    args += [decay, decay]
    if states is not None:
        in_specs += [state_spec, state_spec]
        args += list(states)
    out_specs, out_shape = [], []
    if emit_out:
        out_specs.append(pl.BlockSpec((seq_len, RET_DV), lambda b, h: (b, h)))
        out_shape.append(jax.ShapeDtypeStruct((m, RET_HEADS * RET_DV), BF16))
    if emit_state:
        out_specs += [state_spec, state_spec]
        out_shape += [jax.ShapeDtypeStruct((batch, RET_HEADS, RET_DK, RET_DV), F32)] * 2
    n_chunks = seq_len // chunk
    state = (RET_DK, RET_DV)
    scratch = [pltpu.VMEM(state, F32), pltpu.VMEM(state, F32),
               pltpu.VMEM((4, chunk, RET_DK), F32),
               pltpu.VMEM((n_chunks,) + state, F32),
               pltpu.VMEM((n_chunks,) + state, F32)]
    if emit_out:
        scratch += [pltpu.VMEM((chunk, chunk), F32),
                    pltpu.VMEM((n_chunks,) + state, BF16),
                    pltpu.VMEM((n_chunks,) + state, BF16)]
    outs = pl.pallas_call(
        functools.partial(_ret_kernel, seq_len=seq_len, chunk=chunk, zero_init=states is None,
                          emit_out=emit_out, emit_state=emit_state),
        grid=(batch, RET_HEADS),
        in_specs=in_specs,
        out_specs=out_specs,
        out_shape=out_shape,
        scratch_shapes=scratch,
        compiler_params=_params("arbitrary", "arbitrary"),
        name="retention_%d" % seq_len,
    )(*args)
    outs = list(outs)
    y = outs.pop(0) if emit_out else None
    st = tuple(outs) if emit_state else None
    return y, st


def _att_kernel(*refs, n_lat, tk, lam_init):
    refs = list(refs)
    q_ref, kc_ref, vc_ref = refs.pop(0), refs.pop(0), refs.pop(0)
    kl_ref = vl_ref = None
    if n_lat:
        kl_ref, vl_ref = refs.pop(0), refs.pop(0)
    g_ref, lam_ref, o_ref = refs.pop(0), refs.pop(0), refs.pop(0)
    vt_ref, pt_ref, mc_ref, lc_ref, acc_ref = refs

    tq = q_ref.shape[0]
    t_ctx = kc_ref.shape[0]
    n_chunks = 1 + n_lat
    sub = 8
    contract_last = (((1,), (1,)), ((), ()))

    def span(c):
        return (0, t_ctx) if c == 0 else (t_ctx + (c - 1) * tk, t_ctx + c * tk)

    def keys(c):
        return kc_ref[...] if c == 0 else kl_ref[(c - 1) * tk:c * tk, :]

    @pl.when(pl.program_id(2) == 0)
    def _():
        vt_ref[:, 0:t_ctx] = vc_ref[...].astype(F32).T.astype(BF16)
        for c in range(1, n_chunks):
            lo, hi = span(c)
            vt_ref[:, lo:hi] = vl_ref[(c - 1) * tk:c * tk, :].astype(F32).T.astype(BF16)

    q = q_ref[...]
    lane = lax.broadcasted_iota(jnp.int32, q.shape, 1)
    zero = jnp.zeros_like(q)
    qm = (jnp.where(lane < ATT_DH, q, zero), jnp.where(lane >= ATT_DH, q, zero))

    kb, qb = ATT_STAT_KEYS, ATT_STAT_QUERIES
    for c in range(n_chunks):
        lo, hi = span(c)
        k = keys(c)
        for j in range(2):
            for q0 in range(0, tq, qb):
                st = lax.dot_general(k, qm[j][q0:q0 + qb, :], contract_last,
                                     preferred_element_type=F32)
                for r0 in range(0, hi - lo, kb):
                    blk = st[r0:r0 + kb, :].reshape(kb // sub, sub, qb)
                    m8 = jnp.max(blk, axis=0)
                    p = jnp.exp2(blk - m8[None])
                    b = (lo + r0) // kb
                    mc_ref[j, b, :, q0:q0 + qb] = m8
                    lc_ref[j, b, :, q0:q0 + qb] = jnp.sum(p, axis=0)
                    pt_ref[j, lo + r0:lo + r0 + kb, q0:q0 + qb] = p.reshape(kb, qb).astype(BF16)

    n_blocks = mc_ref.shape[1]
    denom = []
    for j in range(2):
        m = functools.reduce(jnp.maximum, [mc_ref[j, b] for b in range(n_blocks)])
        m = jnp.max(m, axis=0, keepdims=True)
        part = None
        for b in range(n_blocks):
            w = jnp.exp2(mc_ref[j, b] - m)
            mc_ref[j, b] = w
            part = w * lc_ref[j, b] if part is None else part + w * lc_ref[j, b]
        denom.append(jnp.sum(part, axis=0, keepdims=True))
    l0, l1 = denom

    lv = lam_ref[...]
    lam = (jnp.exp(jnp.sum(lv[0:1, :] * lv[1:2, :], axis=-1, keepdims=True))
           - jnp.exp(jnp.sum(lv[2:3, :] * lv[3:4, :], axis=-1, keepdims=True)) + lam_init)
    c1 = lam * l0 / l1

    pack = BF16_SUBLANES
    for c in range(n_chunks):
        lo, hi = span(c)
        parts = []
        for r0 in range(lo, hi, kb):
            w0 = mc_ref[0, r0 // kb]
            w1 = -c1 * mc_ref[1, r0 // kb]
            coef0 = jnp.concatenate([w0, w0], axis=0).astype(BF16)
            coef1 = jnp.concatenate([w1, w1], axis=0).astype(BF16)
            p0 = pt_ref[0, r0:r0 + kb, :].reshape(kb // pack, pack, tq)
            p1 = pt_ref[1, r0:r0 + kb, :].reshape(kb // pack, pack, tq)
            parts.append((p0 * coef0[None] + p1 * coef1[None]).reshape(kb, tq))
        a = parts[0] if len(parts) == 1 else jnp.concatenate(parts, axis=0)
        pv = jnp.dot(vt_ref[:, lo:hi], a, preferred_element_type=F32)
        if c == 0:
            acc_ref[...] = pv
        else:
            acc_ref[...] += pv

    ot = acc_ref[...] / l0
    ms = jnp.mean(ot * ot, axis=0, keepdims=True)
    ot = ot * lax.rsqrt(ms + LN_EPS) * (1.0 - lam_init)
    o_ref[...] = (ot.T * g_ref[...].astype(F32)).astype(BF16)


def _attention(p_q, p_ctx, p_lat, lam_rows, batch, q_len, ctx_len, lat_len, lam_init):
    m = batch * q_len
    tq = min(ATT_TQ, q_len)
    nq = q_len // tq
    tk = ATT_TK
    n_lat = 0 if p_lat is None else lat_len // tk
    n_keys = ctx_len + n_lat * tk
    width = 2 * ATT_DH

    def q_spec(g):
        return pl.BlockSpec((None, tq, width), lambda b, h, i: (g, b * nq + i, h))

    def kv_spec(g, length):
        return pl.BlockSpec((None, length, width), lambda b, h, i: (g, b, h))

    in_specs = [q_spec(G_AQ), kv_spec(G_AK, ctx_len), kv_spec(G_AV, ctx_len)]
    args = [p_q, p_ctx, p_ctx]
    if n_lat:
        in_specs += [kv_spec(G_AK, lat_len), kv_spec(G_AV, lat_len)]
        args += [p_lat, p_lat]
    in_specs += [q_spec(G_AG), pl.BlockSpec(lam_rows.shape, lambda b, h, i: (0, 0))]
    args += [p_q, lam_rows]
    n_chunks = 1 + n_lat
    scratch = [pltpu.VMEM((width, n_keys), BF16),
               pltpu.VMEM((2, n_keys, tq), BF16),
               pltpu.VMEM((2, n_keys // ATT_STAT_KEYS, 8, tq), F32),
               pltpu.VMEM((2, n_keys // ATT_STAT_KEYS, 8, tq), F32),
               pltpu.VMEM((width, tq), F32)]
    return pl.pallas_call(
        functools.partial(_att_kernel, n_lat=n_lat, tk=tk, lam_init=lam_init),
        grid=(batch, ATT_HEADS, nq),
        in_specs=in_specs,
        out_specs=pl.BlockSpec((tq, width), lambda b, h, i: (b * nq + i, h)),
        out_shape=jax.ShapeDtypeStruct((m, ATT_HEADS * width), BF16),
        scratch_shapes=scratch,
        compiler_params=_params("arbitrary", "arbitrary", "arbitrary"),
        name="diff_attention_%d" % q_len,
    )(*args)


def _merge_kernel(yr_ref, ya_ref, cb_ref, cc_ref, cx_ref, cg_ref, ccp_ref, cxp_ref, ccn_ref, cxn_ref,
                  gr_ref, gc_ref, ga_ref, x_ref, gate_ref, cw_ref, lng_ref, lnb_ref,
                  wr_ref, wc_ref, wa_ref, wo_ref, o_ref, *, tiles_per_seq):
    i = pl.program_id(0)
    tm = x_ref.shape[0]
    t = i % tiles_per_seq
    u = cc_ref[...].astype(F32) * cx_ref[...].astype(F32)
    halo = BF16_SUBLANES
    u_before = ccp_ref[halo - 1:halo, :].astype(F32) * cxp_ref[halo - 1:halo, :].astype(F32)
    u_after = ccn_ref[0:1, :].astype(F32) * cxn_ref[0:1, :].astype(F32)
    u_before = jnp.where(t == 0, 0.0, u_before)
    u_after = jnp.where(t == tiles_per_seq - 1, 0.0, u_after)
    row = lax.broadcasted_iota(jnp.int32, u.shape, 0)
    u_prev = jnp.where(row == 0, u_before, pltpu.roll(u, 1, axis=0))
    u_next = jnp.where(row == tm - 1, u_after, pltpu.roll(u, tm - 1, axis=0))
    cw = cw_ref[...]
    conv = cw[0:1, :] * u_prev + cw[1:2, :] * u + cw[2:3, :] * u_next
    y_conv = (cb_ref[...].astype(F32) * conv * cg_ref[...].astype(F32)).astype(BF16)

    n_parts = 2 if tm % (2 * BF16_SUBLANES) == 0 else 1
    for part in range(n_parts):
        r = slice(part * tm // n_parts, (part + 1) * tm // n_parts)
        mix = gr_ref[r, :].astype(F32) * jnp.dot(yr_ref[r, :], wr_ref[...], preferred_element_type=F32)
        mix = mix + gc_ref[r, :].astype(F32) * jnp.dot(y_conv[r, :], wc_ref[...], preferred_element_type=F32)
        mix = mix + ga_ref[r, :].astype(F32) * jnp.dot(ya_ref[r, :], wa_ref[...], preferred_element_type=F32)
        out = jnp.dot(mix.astype(BF16), wo_ref[...], preferred_element_type=F32)
        z = DEEPNORM_ALPHA * x_ref[r, :] + gate_ref[...] * out
        o_ref[r, :] = _layer_norm_rows(z) * lng_ref[...] + lnb_ref[...]


def _merge(p, y_ret, y_att, x2d, mod_rows, mod_row_of_tile, conv_w, ln_g, ln_b, w_r, w_c, w_a, w_o, seq_len):
    m, d = x2d.shape
    tm = min(MERGE_TM, seq_len)
    tiles_per_seq = seq_len // tm
    halo = BF16_SUBLANES
    hb = tm // halo
    n_halo = m // halo

    def group_spec(g):
        return pl.BlockSpec((None, tm, d), lambda i: (g, i, 0))

    def before_spec(g):
        return pl.BlockSpec((None, halo, d), lambda i: (g, jnp.maximum(i * hb - 1, 0), 0))

    def after_spec(g):
        return pl.BlockSpec((None, halo, d), lambda i: (g, jnp.minimum((i + 1) * hb, n_halo - 1), 0))

    row_spec = pl.BlockSpec((tm, d), lambda i: (i, 0))
    vec_spec = pl.BlockSpec((1, d), lambda i: (0, 0))
    w_spec = pl.BlockSpec((d, d), lambda i: (0, 0), pipeline_mode=pl.Buffered(1))
    in_specs = [row_spec, row_spec,
                group_spec(G_CB), group_spec(G_CC), group_spec(G_CX), group_spec(G_CG),
                before_spec(G_CC), before_spec(G_CX), after_spec(G_CC), after_spec(G_CX),
                group_spec(G_MR), group_spec(G_MC), group_spec(G_MA),
                row_spec,
                pl.BlockSpec((None, 1, d), lambda i: (mod_row_of_tile(i), 0, 2)),
                pl.BlockSpec((3, d), lambda i: (0, 0)), vec_spec, vec_spec,
                w_spec, w_spec, w_spec, w_spec]
    args = [y_ret, y_att, p, p, p, p, p, p, p, p, p, p, p, x2d, mod_rows, conv_w,
            ln_g.reshape(1, d), ln_b.reshape(1, d), w_r, w_c, w_a, w_o]
    return pl.pallas_call(
        functools.partial(_merge_kernel, tiles_per_seq=tiles_per_seq),
        grid=(m // tm,),
        in_specs=in_specs,
        out_specs=row_spec,
        out_shape=jax.ShapeDtypeStruct((m, d), F32),
        compiler_params=_params("arbitrary"),
        name="merge_%d" % seq_len,
    )(*args)


def _rope_tables(seq_len):
    pos = jnp.arange(seq_len, dtype=jnp.int32)
    rows = (pos // GRID_W).astype(F32)
    cols = (pos % GRID_W).astype(F32)

    def angles(head_dim):
        n_freq = head_dim // 4
        inv = ROPE_BASE ** (-jnp.arange(n_freq, dtype=F32) / n_freq)
        return jnp.concatenate([rows[:, None] * inv, cols[:, None] * inv], axis=-1)

    ang_r = angles(RET_DK)
    ang_a = angles(ATT_DH)
    cos_a, sin_a = jnp.cos(ang_a), jnp.sin(ang_a)
    cos_att = jnp.concatenate([cos_a] * 4, axis=-1)
    sin_att = jnp.concatenate([-sin_a, sin_a, -sin_a, sin_a], axis=-1)
    return jnp.cos(ang_r), jnp.sin(ang_r), cos_att, sin_att


def kernel(x, c, ctx, c_ctx, w_mod, b_mod, w_in, ret_decay, conv_w, diff_lambda,
           w_ret_out, w_conv_out, w_att_out, w_out, ln_g, ln_b):
    bsz, t_lat, d = x.shape
    t_ctx = ctx.shape[1]
    depth = w_mod.shape[0]
    ctx_row = bsz

    c_rows = jnp.concatenate([c, c_ctx[None, :], jnp.zeros((8 - bsz - 1, d), F32)], axis=0)
    mod = _modulation(c_rows, w_mod, b_mod)
    rope_tabs = _rope_tables(t_lat)
    w_r, w_c, w_a, w_o = (w.astype(BF16) for w in (w_ret_out, w_conv_out, w_att_out, w_out))
    decay = jnp.broadcast_to(ret_decay.astype(F32).reshape(depth, 2 * RET_HEADS, 1, 1),
                             (depth, 2 * RET_HEADS, 1, LANES))

    ln_tm = min(LN_TM, t_lat)
    lat_tiles_ln = t_lat // ln_tm
    lat_tiles_merge = t_lat // min(MERGE_TM, t_lat)
    x2d = x.reshape(bsz * t_lat, d)
    xc2d = ctx.reshape(bsz * t_ctx, d)
    for l in range(depth):
        last = l == depth - 1
        lam_init = 0.8 - 0.6 * math.exp(-0.3 * l)
        mod_rows = mod[l].reshape(8, 1, 3 * d)
        lam_rows = diff_lambda[l].astype(F32)

        n_ctx_groups = N_KV_GROUPS if last else N_GROUPS
        hc = _ln_modulate(xc2d, mod_rows, lambda i: ctx_row, bsz * t_ctx)
        pc = _projection(hc, w_in, l, n_ctx_groups, t_ctx, None)
        cy_ret, states = _retention(pc, decay[l], bsz, t_ctx, None, emit_out=not last, emit_state=True)

        h = _ln_modulate(x2d, mod_rows, lambda i: i // lat_tiles_ln, ln_tm)
        p = _projection(h, w_in, l, N_GROUPS, t_lat, rope_tabs)
        y_ret, _ = _retention(p, decay[l], bsz, t_lat, states, emit_out=True, emit_state=False)
        y_att = _attention(p, pc, p, lam_rows, bsz, t_lat, t_ctx, t_lat, lam_init)
        x_new = _merge(p, y_ret, y_att, x2d, mod_rows, lambda i: i // lat_tiles_merge, conv_w[l],
                       ln_g[l], ln_b[l], w_r[l], w_c[l], w_a[l], w_o[l], t_lat)
        if not last:
            cy_att = _attention(pc, pc, None, lam_rows, bsz, t_ctx, t_ctx, 0, lam_init)
            xc2d = _merge(pc, cy_ret, cy_att, xc2d, mod_rows, lambda i: ctx_row, conv_w[l],
                          ln_g[l], ln_b[l], w_r[l], w_c[l], w_a[l], w_o[l], t_ctx)
        x2d = x_new
    return x2d.reshape(bsz, t_lat, d)
```

```python
import functools
import math

import jax
import jax.numpy as jnp
from jax import lax
from jax.experimental import pallas as pl
from jax.experimental.pallas import tpu as pltpu

F32 = jnp.float32
BF16 = jnp.bfloat16

D_MODEL = 1024
DEPTH = 4
GRID_W = 64
RET_HEADS = 4
RET_DK = 256
RET_DV = 256
ATT_HEADS = 8
ATT_DH = 64
ROPE_BASE = 10000.0
LN_EPS = 1e-6
DEEPNORM_ALPHA = (2 * DEPTH) ** 0.25
LOG2_E = math.log2(math.e)

G_RK, G_RV, G_AK, G_AV, G_RQ, G_RG, G_AQ, G_AG, G_CB, G_CC, G_CX, G_CG, G_MR, G_MC, G_MA = range(15)
N_GROUPS = 15
N_KV_GROUPS = 4

LANES = 128
BF16_SUBLANES = 16
VMEM_LIMIT = 56 * 1024 * 1024

RET_CHUNK = 256
PROJ_TM = 2048
LN_TM = 1024
ATT_TQ = 1024
ATT_TK = 256
ATT_STAT_KEYS = 128
ATT_STAT_QUERIES = 256
MERGE_TM = 512


def _sigmoid(x):
    return 0.5 * jnp.tanh(0.5 * x) + 0.5


def _layer_norm_rows(x):
    mu = jnp.mean(x, axis=-1, keepdims=True)
    xc = x - mu
    var = jnp.mean(xc * xc, axis=-1, keepdims=True)
    return xc * lax.rsqrt(var + LN_EPS)


def _params(*sem):
    return pltpu.CompilerParams(dimension_semantics=sem, vmem_limit_bytes=VMEM_LIMIT)


def _mod_kernel(c_ref, w_ref, b_ref, o_ref):
    c = c_ref[...]
    s = (c * _sigmoid(c)).astype(BF16)
    o_ref[...] = jnp.dot(s, w_ref[...].astype(BF16), preferred_element_type=F32) + b_ref[...]


def _modulation(c_rows, w_mod, b_mod):
    depth, d, d3 = w_mod.shape
    tn = 1024
    return pl.pallas_call(
        _mod_kernel,
        grid=(depth, d3 // tn),
        in_specs=[
            pl.BlockSpec((8, d), lambda l, j: (0, 0)),
            pl.BlockSpec((None, d, tn), lambda l, j: (l, 0, j)),
            pl.BlockSpec((None, 1, tn), lambda l, j: (l, 0, j)),
        ],
        out_specs=pl.BlockSpec((None, 8, tn), lambda l, j: (l, 0, j)),
        out_shape=jax.ShapeDtypeStruct((depth, 8, d3), F32),
        compiler_params=_params("arbitrary", "arbitrary"),
        name="modulation",
    )(c_rows, w_mod, b_mod.reshape(depth, 1, d3))


def _ln_mod_kernel(x_ref, sh_ref, sc_ref, o_ref):
    y = _layer_norm_rows(x_ref[...])
    o_ref[...] = (y * (1.0 + sc_ref[...]) + sh_ref[...]).astype(BF16)


def _ln_modulate(x2d, mod_rows, mod_row_of_tile, tm):
    m, d = x2d.shape
    return pl.pallas_call(
        _ln_mod_kernel,
        grid=(m // tm,),
        in_specs=[pl.BlockSpec((tm, d), lambda i: (i, 0)),
                  pl.BlockSpec((None, 1, d), lambda i: (mod_row_of_tile(i), 0, 0)),
                  pl.BlockSpec((None, 1, d), lambda i: (mod_row_of_tile(i), 0, 1))],
        out_specs=pl.BlockSpec((tm, d), lambda i: (i, 0)),
        out_shape=jax.ShapeDtypeStruct((m, d), BF16),
        compiler_params=_params("arbitrary"),
        name="ln_modulate",
    )(x2d, mod_rows, mod_rows)


def _proj_kernel(*refs, rope):
    if rope:
        h_ref, w_ref, cr_ref, sr_ref, ca_ref, sa_ref, o_ref = refs
    else:
        h_ref, w_ref, o_ref = refs
    j = pl.program_id(0)

    def product():
        return jnp.dot(h_ref[...], w_ref[...].astype(BF16), preferred_element_type=F32)

    def store_rope_ret(scale):
        acc = product().astype(BF16)
        c = (cr_ref[...] * scale).astype(BF16)
        s = (sr_ref[...] * scale).astype(BF16)
        for hh in range(RET_HEADS):
            lo = hh * RET_DK
            mid = lo + RET_DK // 2
            hi = lo + RET_DK
            x1 = acc[:, lo:mid]
            x2 = acc[:, mid:hi]
            o_ref[:, lo:mid] = x1 * c - x2 * s
            o_ref[:, mid:hi] = x1 * s + x2 * c

    def store_rope_att(scale):
        acc = product()
        c = ca_ref[...]
        s = sa_ref[...]
        lane = lax.broadcasted_iota(jnp.int32, c.shape, 1)
        first_half = (lane % ATT_DH) < (ATT_DH // 2)
        for hh in range(ATT_HEADS):
            lo = hh * LANES
            x = acc[:, lo:lo + LANES]
            partner = jnp.where(first_half,
                                pltpu.roll(x, LANES - ATT_DH // 2, axis=1),
                                pltpu.roll(x, ATT_DH // 2, axis=1))
            o_ref[:, lo:lo + LANES] = ((x * c + partner * s) * scale).astype(BF16)

    def store_scaled(scale):
        o_ref[...] = (product() * scale).astype(BF16)

    def store_plain():
        o_ref[...] = product().astype(BF16)

    def store_silu():
        a = product().astype(BF16)
        o_ref[...] = a * _sigmoid(a)

    def store_sigmoid():
        o_ref[...] = _sigmoid(product().astype(BF16))

    q_r_scale = RET_DK ** -0.5
    q_a_scale = ATT_DH ** -0.5 * LOG2_E
    if rope:
        pl.when(j == G_RK)(functools.partial(store_rope_ret, 1.0))
        pl.when(j == G_RQ)(functools.partial(store_rope_ret, q_r_scale))
        pl.when(j == G_AK)(functools.partial(store_rope_att, 1.0))
        pl.when(j == G_AQ)(functools.partial(store_rope_att, q_a_scale))
        plain = (G_RV, G_AV, G_CB, G_CC, G_CX)
    else:
        pl.when(j == G_RQ)(functools.partial(store_scaled, q_r_scale))
        pl.when(j == G_AQ)(functools.partial(store_scaled, q_a_scale))
        plain = (G_RK, G_AK, G_RV, G_AV, G_CB, G_CC, G_CX)
    is_plain = functools.reduce(jnp.logical_or, [j == g for g in plain])
    pl.when(is_plain)(store_plain)
    is_silu = functools.reduce(jnp.logical_or, [j == g for g in (G_RG, G_AG, G_CG)])
    pl.when(is_silu)(store_silu)
    pl.when(j >= G_MR)(store_sigmoid)


def _projection(h2d, w_stack, layer, n_groups, seq_len, rope_tabs):
    m, d = h2d.shape
    rope = rope_tabs is not None
    tm = min(PROJ_TM, seq_len if rope else m)
    tiles_per_seq = max(seq_len // tm, 1)
    in_specs = [
        pl.BlockSpec((tm, d), lambda j, i: (i, 0)),
        pl.BlockSpec((None, d, d), lambda j, i: (layer, 0, j)),
    ]
    args = [h2d, w_stack]
    if rope:
        tab_spec = pl.BlockSpec((tm, LANES), lambda j, i: (i % tiles_per_seq, 0))
        in_specs += [tab_spec] * 4
        args += list(rope_tabs)
    return pl.pallas_call(
        functools.partial(_proj_kernel, rope=rope),
        grid=(n_groups, m // tm),
        in_specs=in_specs,
        out_specs=pl.BlockSpec((None, tm, d), lambda j, i: (j, i, 0)),
        out_shape=jax.ShapeDtypeStruct((n_groups, m, d), BF16),
        compiler_params=_params("arbitrary", "arbitrary"),
        name="projection_rope" if rope else "projection_ctx",
    )(*args)


def _lanes(x, n):
    return x if n == LANES else jnp.concatenate([x] * (n // LANES), axis=1)


_KDEC_F, _KDEC_B, _QDEC_F, _QDEC_B = range(4)


def _ret_kernel(*refs, seq_len, chunk, zero_init, emit_out, emit_state):
    refs = list(refs)
    k_ref, v_ref = refs.pop(0), refs.pop(0)
    q_ref = g_ref = None
    if emit_out:
        q_ref, g_ref = refs.pop(0), refs.pop(0)
    df_ref, db_ref = refs.pop(0), refs.pop(0)
    sf0_ref = sb0_ref = None
    if not zero_init:
        sf0_ref, sb0_ref = refs.pop(0), refs.pop(0)
    y_ref = sfo_ref = sbo_ref = None
    if emit_out:
        y_ref = refs.pop(0)
    if emit_state:
        sfo_ref, sbo_ref = refs.pop(0), refs.pop(0)
    sf_ref, sb_ref, dec_ref, incf_ref, incb_ref = (refs.pop(0) for _ in range(5))
    dmat_ref = sfs_ref = sbs_ref = None
    if emit_out:
        dmat_ref, sfs_ref, sbs_ref = refs.pop(0), refs.pop(0), refs.pop(0)

    n_chunks = seq_len // chunk
    contract_rows = (((0,), (0,)), ((), ()))
    contract_last = (((1,), (1,)), ((), ()))

    lg_f = _lanes(-jnp.exp(df_ref[...]), RET_DK)
    lg_b = _lanes(-jnp.exp(db_ref[...]), RET_DK)
    pos = lax.broadcasted_iota(jnp.int32, (chunk, RET_DK), 0).astype(F32)
    dec_ref[_KDEC_F] = jnp.exp(lg_f * (chunk - 1.0 - pos))
    dec_ref[_KDEC_B] = jnp.exp(lg_b * pos)
    gc_f = jnp.exp(lg_f * float(chunk))
    gc_b = jnp.exp(lg_b * float(chunk))
    if emit_out:
        dec_ref[_QDEC_F] = jnp.exp(lg_f * (pos + 1.0))
        dec_ref[_QDEC_B] = jnp.exp(lg_b * (float(chunk) - pos))
        row = lax.broadcasted_iota(jnp.int32, (chunk, chunk), 0)
        col = lax.broadcasted_iota(jnp.int32, (chunk, chunk), 1)
        dist = (row - col).astype(F32)
        lgf_c = _lanes(lg_f[:, :LANES], chunk)
        lgb_c = _lanes(lg_b[:, :LANES], chunk)
        dmat_ref[...] = jnp.where(dist >= 0.0,
                                  jnp.exp(lgf_c * jnp.maximum(dist, 0.0)),
                                  jnp.exp(lgb_c * jnp.maximum(-dist, 0.0)))

    if zero_init:
        sf_ref[...] = jnp.zeros_like(sf_ref)
        sb_ref[...] = jnp.zeros_like(sb_ref)
    else:
        sf_ref[...] = sf0_ref[...]
        sb_ref[...] = sb0_ref[...]

    def rows(n):
        return slice(n * chunk, (n + 1) * chunk)

    for n in range(n_chunks):
        kn = k_ref[rows(n), :].astype(F32)
        vn = v_ref[rows(n), :]
        incf_ref[n] = lax.dot_general((kn * dec_ref[_KDEC_F]).astype(BF16), vn, contract_rows,
                                      preferred_element_type=F32)
        incb_ref[n] = lax.dot_general((kn * dec_ref[_KDEC_B]).astype(BF16), vn, contract_rows,
                                      preferred_element_type=F32)

    for n in range(n_chunks):
        if emit_out:
            sfs_ref[n] = sf_ref[...].astype(BF16)
        sf_ref[...] = gc_f * sf_ref[...] + incf_ref[n]
    for n in reversed(range(n_chunks)):
        if emit_out:
            sbs_ref[n] = sb_ref[...].astype(BF16)
        sb_ref[...] = gc_b * sb_ref[...] + incb_ref[n]

    if emit_out:
        for n in range(n_chunks):
            r = rows(n)
            qn, kn, vn = q_ref[r, :], k_ref[r, :], v_ref[r, :]
            scores = lax.dot_general(qn, kn, contract_last, preferred_element_type=F32) * dmat_ref[...]
            o = jnp.dot(scores.astype(BF16), vn, preferred_element_type=F32)
            o = o + jnp.dot(qn, sfs_ref[n], preferred_element_type=F32) * dec_ref[_QDEC_F]
            o = o + jnp.dot(qn, sbs_ref[n], preferred_element_type=F32) * dec_ref[_QDEC_B]
            y_ref[r, :] = (_layer_norm_rows(o) * g_ref[r, :].astype(F32)).astype(BF16)

    if emit_state:
        sfo_ref[...] = sf_ref[...]
        sbo_ref[...] = sb_ref[...]


def _retention(p, decay, batch, seq_len, states, emit_out, emit_state):
    m = batch * seq_len
    chunk = min(RET_CHUNK, seq_len)

    def group_spec(g):
        return pl.BlockSpec((None, seq_len, RET_DK), lambda b, h: (g, b, h))

    state_spec = pl.BlockSpec((None, None, RET_DK, RET_DV), lambda b, h: (b, h, 0, 0))
    in_specs = [group_spec(G_RK), group_spec(G_RV)]
    args = [p, p]
    if emit_out:
        in_specs += [group_spec(G_RQ), group_spec(G_RG)]
        args += [p, p]
    in_specs += [pl.BlockSpec((None, 1, LANES), lambda b, h: (h, 0, 0)),
                 pl.BlockSpec((None, 1, LANES), lambda b, h: (RET_HEADS + h, 0, 0))]
    args += [decay, decay]
    if states is not None:
        in_specs += [state_spec, state_spec]
        args += list(states)
    out_specs, out_shape = [], []
    if emit_out:
        out_specs.append(pl.BlockSpec((seq_len, RET_DV), lambda b, h: (b, h)))
        out_shape.append(jax.ShapeDtypeStruct((m, RET_HEADS * RET_DV), BF16))
    if emit_state:
        out_specs += [state_spec, state_spec]
        out_shape += [jax.ShapeDtypeStruct((batch, RET_HEADS, RET_DK, RET_DV), F32)] * 2
    n_chunks = seq_len // chunk
    state = (RET_DK, RET_DV)
    scratch = [pltpu.VMEM(state, F32), pltpu.VMEM(state, F32),
               pltpu.VMEM((4, chunk, RET_DK), F32),
               pltpu.VMEM((n_chunks,) + state, F32),
               pltpu.VMEM((n_chunks,) + state, F32)]
    if emit_out:
        scratch += [pltpu.VMEM((chunk, chunk), F32),
                    pltpu.VMEM((n_chunks,) + state, BF16),
                    pltpu.VMEM((n_chunks,) + state, BF16)]
    outs = pl.pallas_call(
        functools.partial(_ret_kernel, seq_len=seq_len, chunk=chunk, zero_init=states is None,
                          emit_out=emit_out, emit_state=emit_state),
        grid=(batch, RET_HEADS),
        in_specs=in_specs,
        out_specs=out_specs,
        out_shape=out_shape,
        scratch_shapes=scratch,
        compiler_params=_params("arbitrary", "arbitrary"),
        name="retention_%d" % seq_len,
    )(*args)
    outs = list(outs)
    y = outs.pop(0) if emit_out else None
    st = tuple(outs) if emit_state else None
    return y, st


def _att_kernel(*refs, n_lat, tk, lam_init):
    refs = list(refs)
    q_ref, kc_ref, vc_ref = refs.pop(0), refs.pop(0), refs.pop(0)
    kl_ref = vl_ref = None
    if n_lat:
        kl_ref, vl_ref = refs.pop(0), refs.pop(0)
    g_ref, lam_ref, o_ref = refs.pop(0), refs.pop(0), refs.pop(0)
    vt_ref, pt_ref, mc_ref, lc_ref, acc_ref = refs

    tq = q_ref.shape[0]
    t_ctx = kc_ref.shape[0]
    n_chunks = 1 + n_lat
    sub = 8
    contract_last = (((1,), (1,)), ((), ()))

    def span(c):
        return (0, t_ctx) if c == 0 else (t_ctx + (c - 1) * tk, t_ctx + c * tk)

    def keys(c):
        return kc_ref[...] if c == 0 else kl_ref[(c - 1) * tk:c * tk, :]

    @pl.when(pl.program_id(2) == 0)
    def _():
        vt_ref[:, 0:t_ctx] = vc_ref[...].astype(F32).T.astype(BF16)
        for c in range(1, n_chunks):
            lo, hi = span(c)
            vt_ref[:, lo:hi] = vl_ref[(c - 1) * tk:c * tk, :].astype(F32).T.astype(BF16)

    q = q_ref[...]
    lane = lax.broadcasted_iota(jnp.int32, q.shape, 1)
    zero = jnp.zeros_like(q)
    qm = (jnp.where(lane < ATT_DH, q, zero), jnp.where(lane >= ATT_DH, q, zero))

    kb, qb = ATT_STAT_KEYS, ATT_STAT_QUERIES
    for c in range(n_chunks):
        lo, hi = span(c)
        k = keys(c)
        for j in range(2):
            for q0 in range(0, tq, qb):
                st = lax.dot_general(k, qm[j][q0:q0 + qb, :], contract_last,
                                     preferred_element_type=F32)
                for r0 in range(0, hi - lo, kb):
                    blk = st[r0:r0 + kb, :].reshape(kb // sub, sub, qb)
                    m8 = jnp.max(blk, axis=0)
                    p = jnp.exp2(blk - m8[None])
                    b = (lo + r0) // kb
                    mc_ref[j, b, :, q0:q0 + qb] = m8
                    lc_ref[j, b, :, q0:q0 + qb] = jnp.sum(p, axis=0)
                    pt_ref[j, lo + r0:lo + r0 + kb, q0:q0 + qb] = p.reshape(kb, qb).astype(BF16)

    n_blocks = mc_ref.shape[1]
    denom = []
    for j in range(2):
        m = functools.reduce(jnp.maximum, [mc_ref[j, b] for b in range(n_blocks)])
        m = jnp.max(m, axis=0, keepdims=True)
        part = None
        for b in range(n_blocks):
            w = jnp.exp2(mc_ref[j, b] - m)
            mc_ref[j, b] = w
            part = w * lc_ref[j, b] if part is None else part + w * lc_ref[j, b]
        denom.append(jnp.sum(part, axis=0, keepdims=True))
    l0, l1 = denom

    lv = lam_ref[...]
    lam = (jnp.exp(jnp.sum(lv[0:1, :] * lv[1:2, :], axis=-1, keepdims=True))
           - jnp.exp(jnp.sum(lv[2:3, :] * lv[3:4, :], axis=-1, keepdims=True)) + lam_init)
    c1 = lam * l0 / l1

    pack = BF16_SUBLANES
    for c in range(n_chunks):
        lo, hi = span(c)
        parts = []
        for r0 in range(lo, hi, kb):
            w0 = mc_ref[0, r0 // kb]
            w1 = -c1 * mc_ref[1, r0 // kb]
            coef0 = jnp.concatenate([w0, w0], axis=0).astype(BF16)
            coef1 = jnp.concatenate([w1, w1], axis=0).astype(BF16)
            p0 = pt_ref[0, r0:r0 + kb, :].reshape(kb // pack, pack, tq)
            p1 = pt_ref[1, r0:r0 + kb, :].reshape(kb // pack, pack, tq)
            parts.append((p0 * coef0[None] + p1 * coef1[None]).reshape(kb, tq))
        a = parts[0] if len(parts) == 1 else jnp.concatenate(parts, axis=0)
        pv = jnp.dot(vt_ref[:, lo:hi], a, preferred_element_type=F32)
        if c == 0:
            acc_ref[...] = pv
        else:
            acc_ref[...] += pv

    ot = acc_ref[...] / l0
    ms = jnp.mean(ot * ot, axis=0, keepdims=True)
    ot = ot * lax.rsqrt(ms + LN_EPS) * (1.0 - lam_init)
    o_ref[...] = (ot.T * g_ref[...].astype(F32)).astype(BF16)


def _attention(p_q, p_ctx, p_lat, lam_rows, batch, q_len, ctx_len, lat_len, lam_init):
    m = batch * q_len
    tq = min(ATT_TQ, q_len)
    nq = q_len // tq
    tk = ATT_TK
    n_lat = 0 if p_lat is None else lat_len // tk
    n_keys = ctx_len + n_lat * tk
    width = 2 * ATT_DH

    def q_spec(g):
        return pl.BlockSpec((None, tq, width), lambda b, h, i: (g, b * nq + i, h))

    def kv_spec(g, length):
        return pl.BlockSpec((None, length, width), lambda b, h, i: (g, b, h))

    in_specs = [q_spec(G_AQ), kv_spec(G_AK, ctx_len), kv_spec(G_AV, ctx_len)]
    args = [p_q, p_ctx, p_ctx]
    if n_lat:
        in_specs += [kv_spec(G_AK, lat_len), kv_spec(G_AV, lat_len)]
        args += [p_lat, p_lat]
    in_specs += [q_spec(G_AG), pl.BlockSpec(lam_rows.shape, lambda b, h, i: (0, 0))]
    args += [p_q, lam_rows]
    n_chunks = 1 + n_lat
    scratch = [pltpu.VMEM((width, n_keys), BF16),
               pltpu.VMEM((2, n_keys, tq), BF16),
               pltpu.VMEM((2, n_keys // ATT_STAT_KEYS, 8, tq), F32),
               pltpu.VMEM((2, n_keys // ATT_STAT_KEYS, 8, tq), F32),
               pltpu.VMEM((width, tq), F32)]
    return pl.pallas_call(
        functools.partial(_att_kernel, n_lat=n_lat, tk=tk, lam_init=lam_init),
        grid=(batch, ATT_HEADS, nq),
        in_specs=in_specs,
        out_specs=pl.BlockSpec((tq, width), lambda b, h, i: (b * nq + i, h)),
        out_shape=jax.ShapeDtypeStruct((m, ATT_HEADS * width), BF16),
        scratch_shapes=scratch,
        compiler_params=_params("arbitrary", "arbitrary", "arbitrary"),
        name="diff_attention_%d" % q_len,
    )(*args)


def _merge_kernel(yr_ref, ya_ref, cb_ref, cc_ref, cx_ref, cg_ref, ccp_ref, cxp_ref, ccn_ref, cxn_ref,
                  gr_ref, gc_ref, ga_ref, x_ref, gate_ref, cw_ref, lng_ref, lnb_ref,
                  wr_ref, wc_ref, wa_ref, wo_ref, *rest, tiles_per_seq):
    if len(rest) == 1:
        (o_ref,) = rest
        nsh_ref = nsc_ref = h_ref = None
    else:
        nsh_ref, nsc_ref, o_ref, h_ref = rest
    i = pl.program_id(0)
    tm = x_ref.shape[0]
    t = i % tiles_per_seq
    u = cc_ref[...].astype(F32) * cx_ref[...].astype(F32)
    halo = BF16_SUBLANES
    u_before = ccp_ref[halo - 1:halo, :].astype(F32) * cxp_ref[halo - 1:halo, :].astype(F32)
    u_after = ccn_ref[0:1, :].astype(F32) * cxn_ref[0:1, :].astype(F32)
    u_before = jnp.where(t == 0, 0.0, u_before)
    u_after = jnp.where(t == tiles_per_seq - 1, 0.0, u_after)
    row = lax.broadcasted_iota(jnp.int32, u.shape, 0)
    u_prev = jnp.where(row == 0, u_before, pltpu.roll(u, 1, axis=0))
    u_next = jnp.where(row == tm - 1, u_after, pltpu.roll(u, tm - 1, axis=0))
    cw = cw_ref[...]
    conv = cw[0:1, :] * u_prev + cw[1:2, :] * u + cw[2:3, :] * u_next
    y_conv = (cb_ref[...].astype(F32) * conv * cg_ref[...].astype(F32)).astype(BF16)

    n_parts = 2 if tm % (2 * BF16_SUBLANES) == 0 else 1
    for part in range(n_parts):
        r = slice(part * tm // n_parts, (part + 1) * tm // n_parts)
        mix = gr_ref[r, :].astype(F32) * jnp.dot(yr_ref[r, :], wr_ref[...], preferred_element_type=F32)
        mix = mix + gc_ref[r, :].astype(F32) * jnp.dot(y_conv[r, :], wc_ref[...], preferred_element_type=F32)
        mix = mix + ga_ref[r, :].astype(F32) * jnp.dot(ya_ref[r, :], wa_ref[...], preferred_element_type=F32)
        out = jnp.dot(mix.astype(BF16), wo_ref[...], preferred_element_type=F32)
        z = DEEPNORM_ALPHA * x_ref[r, :] + gate_ref[...] * out
        x_new = _layer_norm_rows(z) * lng_ref[...] + lnb_ref[...]
        o_ref[r, :] = x_new
        if h_ref is not None:
            h_ref[r, :] = (_layer_norm_rows(x_new) * (1.0 + nsc_ref[...]) + nsh_ref[...]).astype(BF16)


def _merge(p, y_ret, y_att, x2d, mod_rows, mod_row_of_tile, conv_w, ln_g, ln_b, w_r, w_c, w_a, w_o, seq_len,
           next_mod_rows=None):
    m, d = x2d.shape
    tm = min(MERGE_TM, seq_len)
    tiles_per_seq = seq_len // tm
    halo = BF16_SUBLANES
    hb = tm // halo
    n_halo = m // halo

    def group_spec(g):
        return pl.BlockSpec((None, tm, d), lambda i: (g, i, 0))

    def before_spec(g):
        return pl.BlockSpec((None, halo, d), lambda i: (g, jnp.maximum(i * hb - 1, 0), 0))

    def after_spec(g):
        return pl.BlockSpec((None, halo, d), lambda i: (g, jnp.minimum((i + 1) * hb, n_halo - 1), 0))

    row_spec = pl.BlockSpec((tm, d), lambda i: (i, 0))
    vec_spec = pl.BlockSpec((1, d), lambda i: (0, 0))
    w_spec = pl.BlockSpec((d, d), lambda i: (0, 0), pipeline_mode=pl.Buffered(1))
    in_specs = [row_spec, row_spec,
                group_spec(G_CB), group_spec(G_CC), group_spec(G_CX), group_spec(G_CG),
                before_spec(G_CC), before_spec(G_CX), after_spec(G_CC), after_spec(G_CX),
                group_spec(G_MR), group_spec(G_MC), group_spec(G_MA),
                row_spec,
                pl.BlockSpec((None, 1, d), lambda i: (mod_row_of_tile(i), 0, 2)),
                pl.BlockSpec((3, d), lambda i: (0, 0)), vec_spec, vec_spec,
                w_spec, w_spec, w_spec, w_spec]
    args = [y_ret, y_att, p, p, p, p, p, p, p, p, p, p, p, x2d, mod_rows, conv_w,
            ln_g.reshape(1, d), ln_b.reshape(1, d), w_r, w_c, w_a, w_o]
    out_specs, out_shape = row_spec, jax.ShapeDtypeStruct((m, d), F32)
    if next_mod_rows is not None:
        in_specs += [pl.BlockSpec((None, 1, d), lambda i: (mod_row_of_tile(i), 0, 0)),
                     pl.BlockSpec((None, 1, d), lambda i: (mod_row_of_tile(i), 0, 1))]
        args += [next_mod_rows, next_mod_rows]
        out_specs = [row_spec, row_spec]
        out_shape = [out_shape, jax.ShapeDtypeStruct((m, d), BF16)]
    outs = pl.pallas_call(
        functools.partial(_merge_kernel, tiles_per_seq=tiles_per_seq),
        grid=(m // tm,),
        in_specs=in_specs,
        out_specs=out_specs,
        out_shape=out_shape,
        compiler_params=_params("arbitrary"),
        name="merge_%d" % seq_len,
    )(*args)
    return (outs, None) if next_mod_rows is None else tuple(outs)


def _rope_tables(seq_len):
    pos = jnp.arange(seq_len, dtype=jnp.int32)
    rows = (pos // GRID_W).astype(F32)
    cols = (pos % GRID_W).astype(F32)

    def angles(head_dim):
        n_freq = head_dim // 4
        inv = ROPE_BASE ** (-jnp.arange(n_freq, dtype=F32) / n_freq)
        return jnp.concatenate([rows[:, None] * inv, cols[:, None] * inv], axis=-1)

    ang_r = angles(RET_DK)
    ang_a = angles(ATT_DH)
    cos_a, sin_a = jnp.cos(ang_a), jnp.sin(ang_a)
    cos_att = jnp.concatenate([cos_a] * 4, axis=-1)
    sin_att = jnp.concatenate([-sin_a, sin_a, -sin_a, sin_a], axis=-1)
    return jnp.cos(ang_r), jnp.sin(ang_r), cos_att, sin_att


def kernel(x, c, ctx, c_ctx, w_mod, b_mod, w_in, ret_decay, conv_w, diff_lambda,
           w_ret_out, w_conv_out, w_att_out, w_out, ln_g, ln_b):
    bsz, t_lat, d = x.shape
    t_ctx = ctx.shape[1]
    depth = w_mod.shape[0]
    ctx_row = bsz

    c_rows = jnp.concatenate([c, c_ctx[None, :], jnp.zeros((8 - bsz - 1, d), F32)], axis=0)
    mod = _modulation(c_rows, w_mod, b_mod)
    rope_tabs = _rope_tables(t_lat)
    w_r, w_c, w_a, w_o = (w.astype(BF16) for w in (w_ret_out, w_conv_out, w_att_out, w_out))
    decay = jnp.broadcast_to(ret_decay.astype(F32).reshape(depth, 2 * RET_HEADS, 1, 1),
                             (depth, 2 * RET_HEADS, 1, LANES))

    ln_tm = min(LN_TM, t_lat)
    lat_tiles_ln = t_lat // ln_tm
    lat_tiles_merge = t_lat // min(MERGE_TM, t_lat)
    x2d = x.reshape(bsz * t_lat, d)
    xc2d = ctx.reshape(bsz * t_ctx, d)
    mod_rows = mod[0].reshape(8, 1, 3 * d)
    h = _ln_modulate(x2d, mod_rows, lambda i: i // lat_tiles_ln, ln_tm)
    hc = _ln_modulate(xc2d, mod_rows, lambda i: ctx_row, bsz * t_ctx)
    for l in range(depth):
        last = l == depth - 1
        lam_init = 0.8 - 0.6 * math.exp(-0.3 * l)
        mod_rows = mod[l].reshape(8, 1, 3 * d)
        next_rows = None if last else mod[l + 1].reshape(8, 1, 3 * d)
        lam_rows = diff_lambda[l].astype(F32)

        n_ctx_groups = N_KV_GROUPS if last else N_GROUPS
        pc = _projection(hc, w_in, l, n_ctx_groups, t_ctx, None)
        cy_ret, states = _retention(pc, decay[l], bsz, t_ctx, None, emit_out=not last, emit_state=True)

        p = _projection(h, w_in, l, N_GROUPS, t_lat, rope_tabs)
        y_ret, _ = _retention(p, decay[l], bsz, t_lat, states, emit_out=True, emit_state=False)
        y_att = _attention(p, pc, p, lam_rows, bsz, t_lat, t_ctx, t_lat, lam_init)
        x_new, h = _merge(p, y_ret, y_att, x2d, mod_rows, lambda i: i // lat_tiles_merge, conv_w[l],
                          ln_g[l], ln_b[l], w_r[l], w_c[l], w_a[l], w_o[l], t_lat, next_rows)
        if not last:
            cy_att = _attention(pc, pc, None, lam_rows, bsz, t_ctx, t_ctx, 0, lam_init)
            xc2d, hc = _merge(pc, cy_ret, cy_att, xc2d, mod_rows, lambda i: ctx_row, conv_w[l],
                              ln_g[l], ln_b[l], w_r[l], w_c[l], w_a[l], w_o[l], t_ctx, next_rows)
        x2d = x_new
    return x2d.reshape(bsz, t_lat, d)
```
